```python
import math
import jax, jax.numpy as jnp
from jax import lax
import numpy as np

D_MODEL = 2048
BATCH = 4
SEQ = 8192
DEPTH = 4

HEAD_DIM = 128
N_HEADS_SB = 4
N_HEADS_DIL = 4
D_SB = N_HEADS_SB * HEAD_DIM
D_DIL = N_HEADS_DIL * HEAD_DIM
D_MIX = D_SB + D_DIL
D_IN = 3 * D_MIX
DILATED_CONFIGS = ((128, 1), (512, 4), (2048, 16))
SB_BLOCK = 128
D_FF = 5632
D_FF_EXPERT = 1408
N_EXPERTS = 8
TOP_K = 2
ROPE_THETA = 10000.0
EPS = 1e-6
NEG_INF = -1e30
N_DENSE = (DEPTH + 1) // 2
N_MOE = DEPTH // 2

kernel_name = "hymba_stickbreak_dilated_moe_trunk"


def rmsnorm(x, g):
    x32 = x.astype(jnp.float32)
    y = x32 * lax.rsqrt(jnp.mean(x32 * x32, axis=-1, keepdims=True) + EPS)
    return (y * g.astype(jnp.float32)).astype(x.dtype)


def rope_tables(positions):
    inv = ROPE_THETA ** (-jnp.arange(0, HEAD_DIM, 2, dtype=jnp.float32) / HEAD_DIM)
    ang = positions.astype(jnp.float32)[..., None] * inv
    cos = jnp.concatenate([jnp.cos(ang), jnp.cos(ang)], -1)[:, :, None, :]
    sin = jnp.concatenate([jnp.sin(ang), jnp.sin(ang)], -1)[:, :, None, :]
    return cos, sin


def apply_rope(x, cos, sin):
    x32 = x.astype(jnp.float32)
    x1, x2 = jnp.split(x32, 2, axis=-1)
    rot = jnp.concatenate([-x2, x1], axis=-1)
    return (x32 * cos + rot * sin).astype(x.dtype)


def stick_breaking(q, k, v):
    B, S, H, dh = q.shape
    C = SB_BLOCK
    nb = S // C
    scale = 1.0 / math.sqrt(dh)
    qt = q.astype(jnp.float32).transpose(0, 2, 1, 3) * scale
    kt = k.astype(jnp.float32).transpose(0, 2, 1, 3)
    vt = v.astype(jnp.float32).transpose(0, 2, 1, 3)
    ar = jnp.arange(C)
    incl = (ar[:, None] >= ar[None, :]).astype(jnp.float32)
    strict = ar[:, None] > ar[None, :]
    outs = []
    for n in range(nb):
        qn = qt[:, :, n * C:(n + 1) * C]
        z_d = jnp.einsum('bhtd,bhsd->bhts', qn, kt[:, :, n * C:(n + 1) * C])
        sp_d = jnp.where(strict, jnp.log1p(jnp.exp(z_d)), 0.0)
        r_d = jnp.einsum('bhtj,js->bhts', sp_d, incl)
        a_d = jnp.where(strict, jnp.exp(z_d - r_d), 0.0)
        o = jnp.einsum('bhts,bhsd->bhtd', a_d, vt[:, :, n * C:(n + 1) * C])
        if n > 0:
            ko = kt[:, :, :n * C].reshape(B, H, n, C, dh)
            vo = vt[:, :, :n * C].reshape(B, H, n, C, dh)
            z_o = jnp.einsum('bhtd,bhcsd->bhtcs', qn, ko)
            sp_o = jnp.log1p(jnp.exp(z_o))
            r_o = jnp.einsum('bhtcj,js->bhtcs', sp_o, incl)
            tot = r_o[..., 0]
            later = lax.cumsum(tot, axis=3, reverse=True) - tot + r_d[..., :1]
            a_o = jnp.exp(z_o - r_o - later[..., None])
            o = o + jnp.einsum('bhtcs,bhcsd->bhtd', a_o, vo)
        outs.append(o)
    o = jnp.concatenate(outs, axis=2)
    return o.transpose(0, 2, 1, 3).reshape(B, S, H * dh).astype(q.dtype)


def dilated_branch(q, k, v, steps, dil):
    B, S, H, dh = q.shape
    span = steps * dil
    Sp = -(-S // span) * span
    pad = [(0, 0), (0, Sp - S), (0, 0), (0, 0)]
    L = Sp // dil
    nb = L // steps
    scale = 1.0 / math.sqrt(dh)

    def regroup(t):
        t = jnp.pad(t.astype(jnp.float32), pad)
        return t.reshape(B, L, dil, H, dh).transpose(0, 2, 1, 3, 4).reshape(B, dil, nb, steps, H, dh)

    def with_prev(t):
        prev = jnp.concatenate([jnp.zeros_like(t[:, :, :1]), t[:, :, :-1]], axis=2)
        return jnp.concatenate([prev, t], axis=3)

    qs = regroup(q)
    kk = with_prev(regroup(k))
    vv = with_prev(regroup(v))
    s = jnp.einsum('brnqhd,brnkhd->brnhqk', qs, kk) * scale
    qi = jnp.arange(steps)[:, None]
    ki = jnp.arange(2 * steps)[None, :]
    dist = qi + steps - ki
    band = (dist >= 0) & (dist <= steps)
    has_prev = (jnp.arange(nb)[:, None, None] > 0) | (ki[None] >= steps)
    mask = (band[None] & has_prev)[:, None]
    s = jnp.where(mask, s, NEG_INF)
    m = jnp.max(s, axis=-1, keepdims=True)
    p = jnp.exp(s - m)
    den = jnp.sum(p, axis=-1, keepdims=True)
    o = jnp.einsum('brnhqk,brnkhd->brnqhd', p / den, vv)
    lse = (m + jnp.log(den))[..., 0]
    o = o.reshape(B, dil, L, H, dh).transpose(0, 2, 1, 3, 4).reshape(B, Sp, H, dh)[:, :S]
    lse = lse.transpose(0, 1, 2, 4, 3).reshape(B, dil, L, H).transpose(0, 2, 1, 3).reshape(B, Sp, H)[:, :S]
    return o, lse


def dilated_attention(q, k, v):
    B, S, H, dh = q.shape
    outs, lses = [], []
    for window, dil in DILATED_CONFIGS:
        o, l = dilated_branch(q, k, v, window // dil, dil)
        outs.append(o)
        lses.append(l)
    alpha = jax.nn.softmax(jnp.stack(lses, 0), axis=0)
    o = jnp.einsum('gbsh,gbshd->bshd', alpha, jnp.stack(outs, 0))
    return o.reshape(B, S, H * dh).astype(q.dtype)


def mixer_layer(x, cos, sin, g_norm, w_in, q_norm_sb, k_norm_sb, q_norm_dil, k_norm_dil,
                g_out_sb, g_out_dil, w_out):
    B, S, _ = x.shape
    h = rmsnorm(x, g_norm) @ w_in
    qa, ka, va, qb, kb, vb = jnp.split(
        h, [D_SB, 2 * D_SB, 3 * D_SB, 3 * D_SB + D_DIL, 3 * D_SB + 2 * D_DIL], axis=-1)
    heads = lambda t: t.reshape(B, S, -1, HEAD_DIM)
    o_sb = stick_breaking(rmsnorm(heads(qa), q_norm_sb), rmsnorm(heads(ka), k_norm_sb),
                          heads(va))
    qb = apply_rope(rmsnorm(heads(qb), q_norm_dil), cos, sin)
    kb = apply_rope(rmsnorm(heads(kb), k_norm_dil), cos, sin)
    o_dil = dilated_attention(qb, kb, heads(vb))
    o = jnp.concatenate([rmsnorm(o_sb, g_out_sb), rmsnorm(o_dil, g_out_dil)], axis=-1)
    return x + o @ w_out


def swiglu(x, wg, wu, wd):
    return (jax.nn.silu(x @ wg) * (x @ wu)) @ wd


def moe(xn, w_router, wg, wu, wd):
    B, S, D = xn.shape
    t = xn.reshape(-1, D)
    logits = (t @ w_router).astype(jnp.float32)
    top_vals, top_idx = lax.top_k(logits, TOP_K)
    gates = jax.nn.softmax(top_vals, axis=-1)
    comb = jnp.einsum('nk,nke->ne', gates, jax.nn.one_hot(top_idx, N_EXPERTS, dtype=jnp.float32))
    y = jnp.zeros_like(t)
    for e in range(N_EXPERTS):
        y = y + comb[:, e:e + 1].astype(t.dtype) * swiglu(t, wg[e], wu[e], wd[e])
    return y.reshape(B, S, D)


def setup_inputs(seed: int = 0) -> dict:
    key = jax.random.key(seed)
    ks = jax.random.split(key, 20)
    f32 = jnp.float32
    nrm = lambda k, shape, fan_in: jax.random.normal(k, shape, f32) * fan_in ** -0.5
    gain = lambda k, shape: 1.0 + 0.02 * jax.random.normal(k, shape, f32)
    x = jax.random.normal(ks[0], (BATCH, SEQ, D_MODEL), f32)
    positions = jnp.broadcast_to(jnp.arange(SEQ, dtype=jnp.int32)[None, :], (BATCH, SEQ))
    return {
        "x": x,
        "positions": positions,
        "norm_mix": gain(ks[1], (DEPTH, D_MODEL)),
        "w_in": nrm(ks[2], (DEPTH, D_MODEL, D_IN), D_MODEL),
        "q_norm_sb": gain(ks[3], (DEPTH, HEAD_DIM)),
        "k_norm_sb": gain(ks[4], (DEPTH, HEAD_DIM)),
        "q_norm_dil": gain(ks[5], (DEPTH, HEAD_DIM)),
        "k_norm_dil": gain(ks[6], (DEPTH, HEAD_DIM)),
        "out_norm_sb": gain(ks[7], (DEPTH, D_SB)),
        "out_norm_dil": gain(ks[8], (DEPTH, D_DIL)),
        "w_out": nrm(ks[9], (DEPTH, D_MIX, D_MODEL), D_MIX),
        "norm_ffn": gain(ks[10], (DEPTH, D_MODEL)),
        "w_gate_dense": nrm(ks[11], (N_DENSE, D_MODEL, D_FF), D_MODEL),
        "w_up_dense": nrm(ks[12], (N_DENSE, D_MODEL, D_FF), D_MODEL),
        "w_down_dense": nrm(ks[13], (N_DENSE, D_FF, D_MODEL), D_FF),
        "w_router": nrm(ks[14], (N_MOE, D_MODEL, N_EXPERTS), D_MODEL),
        "w_gate_moe": nrm(ks[15], (N_MOE, N_EXPERTS, D_MODEL, D_FF_EXPERT), D_MODEL),
        "w_up_moe": nrm(ks[16], (N_MOE, N_EXPERTS, D_MODEL, D_FF_EXPERT), D_MODEL),
        "w_down_moe": nrm(ks[17], (N_MOE, N_EXPERTS, D_FF_EXPERT, D_MODEL), D_FF_EXPERT),
    }


def reference(x, positions, norm_mix, w_in, q_norm_sb, k_norm_sb, q_norm_dil, k_norm_dil,
              out_norm_sb, out_norm_dil, w_out, norm_ffn, w_gate_dense, w_up_dense, w_down_dense,
              w_router, w_gate_moe, w_up_moe, w_down_moe):
    cos, sin = rope_tables(positions)
    for i in range(DEPTH):
        x = mixer_layer(x, cos, sin, norm_mix[i], w_in[i], q_norm_sb[i], k_norm_sb[i],
                        q_norm_dil[i], k_norm_dil[i], out_norm_sb[i], out_norm_dil[i], w_out[i])
        xn = rmsnorm(x, norm_ffn[i])
        j = i // 2
        if i % 2 == 0:
            x = x + swiglu(xn, w_gate_dense[j], w_up_dense[j], w_down_dense[j])
        else:
            x = x + moe(xn, w_router[j], w_gate_moe[j], w_up_moe[j], w_down_moe[j])
    return x
```

```python
import functools
import math

import jax
import jax.numpy as jnp
from jax import lax
from jax.experimental import pallas as pl
from jax.experimental.pallas import tpu as pltpu

HEAD_DIM = 128
N_HEADS_SB = 4
N_HEADS_DIL = 4
N_SLABS = 3 * (N_HEADS_SB + N_HEADS_DIL)
SLAB_Q_SB, SLAB_K_SB, SLAB_V_SB = 0, 4, 8
SLAB_Q_DIL, SLAB_K_DIL, SLAB_V_DIL = 12, 16, 20
DILATED_CONFIGS = ((128, 1), (512, 4), (2048, 16))
DIL_STEPS = 128
N_EXPERTS = 8
ROPE_THETA = 10000.0
EPS = 1e-6
NEG_INF = -1e30
SB_ZERO_WEIGHT_LOG = 112.0

V7X_LANES = 128
V7X_VMEM_LIMIT_BYTES = 56 * 1024 * 1024

F32 = jnp.float32
BF16 = jnp.bfloat16


def _params(*semantics):
    return pltpu.CompilerParams(dimension_semantics=semantics,
                                vmem_limit_bytes=V7X_VMEM_LIMIT_BYTES)


def _dot(a, b):
    return jnp.dot(a, b, preferred_element_type=F32)


def _dot_nt(a, b):
    return lax.dot_general(a, b, (((1,), (1,)), ((), ())), preferred_element_type=F32)


def _rope_kernel(pos_ref, cos_ref, sin_ref):
    lane = lax.broadcasted_iota(jnp.int32, pos_ref.shape, 1)
    half = HEAD_DIM // 2
    pair = (lane % half).astype(F32)
    inv = jnp.exp(pair * (-2.0 * math.log(ROPE_THETA) / HEAD_DIM))
    ang = pos_ref[...].astype(F32) * inv
    cos_ref[...] = jnp.cos(ang)
    s = jnp.sin(ang)
    sin_ref[...] = jnp.where(lane < half, -s, s)


def rope_tables(pos_b, tm=1024):
    n = pos_b.shape[0]
    spec = pl.BlockSpec((tm, HEAD_DIM), lambda i: (i, 0))
    return pl.pallas_call(
        _rope_kernel,
        grid=(n // tm,),
        in_specs=[spec],
        out_specs=[spec, spec],
        out_shape=[jax.ShapeDtypeStruct((n, HEAD_DIM), F32)] * 2,
        compiler_params=_params("parallel"),
    )(pos_b)


def _inproj_kernel(x_ref, g_ref, w_ref, gains_ref, cos_ref, sin_ref, o_ref):
    x = x_ref[...]
    ms = jnp.mean(x * x, axis=-1, keepdims=True)
    xn = (x * lax.rsqrt(ms + EPS) * g_ref[...]).astype(BF16)
    cos = cos_ref[...]
    sin = sin_ref[...]
    scale = 1.0 / math.sqrt(HEAD_DIM)
    group = 4
    for j0 in range(0, N_SLABS, group):
        h = _dot(xn, w_ref[:, j0 * HEAD_DIM:(j0 + group) * HEAD_DIM])
        for j in range(j0, j0 + group):
            y = h[:, (j - j0) * HEAD_DIM:(j - j0 + 1) * HEAD_DIM]
            is_v = SLAB_V_SB <= j < SLAB_Q_DIL or j >= SLAB_V_DIL
            if not is_v:
                ms_h = jnp.mean(y * y, axis=-1, keepdims=True)
                y = y * lax.rsqrt(ms_h + EPS) * gains_ref[j:j + 1, :]
                if j >= SLAB_Q_DIL:
                    y = y * cos + pltpu.roll(y, HEAD_DIM // 2, 1) * sin
                if j < SLAB_K_SB or SLAB_Q_DIL <= j < SLAB_K_DIL:
                    y = y * scale
            o_ref[0, j] = y.astype(BF16)


def in_projection(x, g_norm, w_in, gains, cos, sin, batch, tm=512):
    n, d = x.shape
    s = n // batch
    d_in = w_in.shape[1]
    nt = s // tm
    return pl.pallas_call(
        _inproj_kernel,
        grid=(batch, nt),
        in_specs=[
            pl.BlockSpec((tm, d), lambda b, i: (b * nt + i, 0)),
            pl.BlockSpec((1, d), lambda b, i: (0, 0)),
            pl.BlockSpec((d, d_in), lambda b, i: (0, 0)),
            pl.BlockSpec((N_SLABS, HEAD_DIM), lambda b, i: (0, 0)),
            pl.BlockSpec((tm, HEAD_DIM), lambda b, i: (b * nt + i, 0)),
            pl.BlockSpec((tm, HEAD_DIM), lambda b, i: (b * nt + i, 0)),
        ],
        out_specs=pl.BlockSpec((1, N_SLABS, tm, HEAD_DIM), lambda b, i: (b, 0, i, 0)),
        out_shape=jax.ShapeDtypeStruct((batch, N_SLABS, s, HEAD_DIM), BF16),
        compiler_params=_params("parallel", "parallel"),
    )(x, g_norm, w_in, gains, cos, sin)


def _softplus(z):
    return jnp.maximum(z, 0.0) + jnp.log1p(jnp.exp(-jnp.abs(z)))


def _sb_kernel(q_ref, k_ref, v_ref, o_ref, *, blk):
    i = pl.program_id(2)
    q = q_ref[0, 0]
    row = lax.broadcasted_iota(jnp.int32, (blk, blk), 0)
    col = lax.broadcasted_iota(jnp.int32, (blk, blk), 1)
    incl = (row >= col).astype(BF16)
    strict = col < row

    def block(c, later, acc, diagonal):
        start = pl.multiple_of(c * blk, blk)
        kb = k_ref[0, 0, pl.ds(start, blk), :]
        vb = v_ref[0, 0, pl.ds(start, blk), :]
        z = _dot_nt(q, kb)
        sp = _softplus(z)
        if diagonal:
            sp = jnp.where(strict, sp, 0.0)
        sp_hi = sp.astype(BF16)
        sp_lo = (sp - sp_hi.astype(F32)).astype(BF16)
        r = _dot(sp_hi, incl) + _dot(sp_lo, incl)
        a = jnp.exp(z - r - later)
        if diagonal:
            a = jnp.where(strict, a, 0.0)
        acc = acc + _dot(a.astype(BF16), vb)
        return later + r[:, :1], acc

    later, acc = block(i, jnp.zeros((blk, 1), F32), jnp.zeros((blk, HEAD_DIM), F32), True)

    def cond(carry):
        c, min_later, _, _ = carry
        return jnp.logical_and(c >= 0, min_later < SB_ZERO_WEIGHT_LOG)

    def body(carry):
        c, _, later, acc = carry
        later, acc = block(c, later, acc, False)
        return c - 1, jnp.min(later), later, acc

    _, _, _, acc = lax.while_loop(cond, body, (i - 1, jnp.min(later), later, acc))
    o_ref[0] = acc


def stick_breaking_attention(qkv, blk=256):
    batch, _, s, dh = qkv.shape
    nh = N_HEADS_SB
    o = pl.pallas_call(
        functools.partial(_sb_kernel, blk=blk),
        grid=(batch, nh, s // blk),
        in_specs=[
            pl.BlockSpec((1, 1, blk, dh), lambda b, h, i: (b, SLAB_Q_SB + h, i, 0)),
            pl.BlockSpec((1, 1, s, dh), lambda b, h, i: (b, SLAB_K_SB + h, 0, 0)),
            pl.BlockSpec((1, 1, s, dh), lambda b, h, i: (b, SLAB_V_SB + h, 0, 0)),
        ],
        out_specs=pl.BlockSpec((1, blk, dh), lambda b, h, i: (b, i, h)),
        out_shape=jax.ShapeDtypeStruct((batch, s, nh * dh), F32),
        compiler_params=_params("parallel", "parallel", "arbitrary"),
    )(qkv, qkv, qkv)
    return o.reshape(batch * s, nh * dh)


def _dil_kernel(q_ref, k_ref, v_ref, o_ref, lse_ref, *, tl):
    i = pl.program_id(3)
    st = DIL_STEPS
    row = lax.broadcasted_iota(jnp.int32, (st, st), 0)
    col = lax.broadcasted_iota(jnp.int32, (st, st), 1)
    mask_cur = col <= row
    mask_prev = col >= row
    for j in range(tl // st):
        g = i * (tl // st) + j
        q = q_ref[0, 0, j * st:(j + 1) * st, :]
        cur = pl.multiple_of(g * st, st)
        prev = pl.multiple_of(jnp.maximum(g - 1, 0) * st, st)
        s_c = _dot_nt(q, k_ref[0, 0, pl.ds(cur, st), :])
        s_p = _dot_nt(q, k_ref[0, 0, pl.ds(prev, st), :])
        s_c = jnp.where(mask_cur, s_c, NEG_INF)
        s_p = jnp.where(mask_prev, s_p, NEG_INF)
        s_p = jnp.where(g > 0, s_p, NEG_INF)
        m = jnp.maximum(jnp.max(s_c, axis=-1, keepdims=True), jnp.max(s_p, axis=-1, keepdims=True))
        p_c = jnp.exp(s_c - m)
        p_p = jnp.exp(s_p - m)
        den = jnp.sum(p_c, axis=-1, keepdims=True) + jnp.sum(p_p, axis=-1, keepdims=True)
        o = (_dot(p_c.astype(BF16), v_ref[0, 0, pl.ds(cur, st), :])
             + _dot(p_p.astype(BF16), v_ref[0, 0, pl.ds(prev, st), :]))
        o_ref[0, j * st:(j + 1) * st, :] = o / den
        lse_ref[0, j * st:(j + 1) * st, :] = jnp.broadcast_to(m + jnp.log(den), (st, HEAD_DIM))


def dilated_attention_branch(qkv, dil, tl_max=1024):
    batch, n_slabs, s, dh = qkv.shape
    nh = N_HEADS_DIL
    sub = s // dil
    tl = min(sub, tl_max)
    view = qkv.reshape(batch, n_slabs, sub, dil * dh)
    out_spec = pl.BlockSpec((1, tl, dh), lambda b, h, r, i: (b, i, r * nh + h))
    out_shape = jax.ShapeDtypeStruct((batch, sub, dil * nh * dh), F32)
    o, lse = pl.pallas_call(
        functools.partial(_dil_kernel, tl=tl),
        grid=(batch, nh, dil, sub // tl),
        in_specs=[
            pl.BlockSpec((1, 1, tl, dh), lambda b, h, r, i: (b, SLAB_Q_DIL + h, i, r)),
            pl.BlockSpec((1, 1, sub, dh), lambda b, h, r, i: (b, SLAB_K_DIL + h, 0, r)),
            pl.BlockSpec((1, 1, sub, dh), lambda b, h, r, i: (b, SLAB_V_DIL + h, 0, r)),
        ],
        out_specs=[out_spec, out_spec],
        out_shape=[out_shape, out_shape],
        compiler_params=_params("parallel", "parallel", "parallel", "arbitrary"),
    )(view, view, view)
    return o.reshape(batch * s, nh * dh), lse.reshape(batch * s, nh * dh)


def _outproj_kernel(*refs, with_router):
    (osb_ref, o1_ref, l1_ref, o2_ref, l2_ref, o3_ref, l3_ref, x_ref,
     gsb_ref, gdil_ref, w_ref, gffn_ref) = refs[:12]
    if with_router:
        wr_ref, xo_ref, xn_ref, comb_ref = refs[12:]
    else:
        xo_ref, xn_ref = refs[12:]

    l1, l2, l3 = l1_ref[...], l2_ref[...], l3_ref[...]
    m = jnp.maximum(jnp.maximum(l1, l2), l3)
    e1, e2, e3 = jnp.exp(l1 - m), jnp.exp(l2 - m), jnp.exp(l3 - m)
    o_dil = (e1 * o1_ref[...] + e2 * o2_ref[...] + e3 * o3_ref[...]) / (e1 + e2 + e3)

    def norm(y, g):
        return y * lax.rsqrt(jnp.mean(y * y, axis=-1, keepdims=True) + EPS) * g

    d_sb = osb_ref.shape[1]
    y_sb = norm(osb_ref[...], gsb_ref[...]).astype(BF16)
    y_dil = norm(o_dil, gdil_ref[...]).astype(BF16)
    x = x_ref[...] + _dot(y_sb, w_ref[:d_sb, :]) + _dot(y_dil, w_ref[d_sb:, :])
    xo_ref[...] = x
    xn = norm(x, gffn_ref[...])
    xn_ref[...] = xn.astype(BF16)

    if with_router:
        lane = lax.broadcasted_iota(jnp.int32, comb_ref.shape, 1)
        logits = jnp.full(comb_ref.shape, -jnp.inf, F32)
        for e in range(N_EXPERTS):
            le = jnp.sum(xn * wr_ref[e:e + 1, :], axis=-1, keepdims=True)
            logits = jnp.where(lane == e, le, logits)
        top1 = jnp.max(logits, axis=-1, keepdims=True)
        idx1 = jnp.min(jnp.where(logits == top1, lane, V7X_LANES), axis=-1, keepdims=True)
        rest = jnp.where(lane == idx1, -jnp.inf, logits)
        top2 = jnp.max(rest, axis=-1, keepdims=True)
        idx2 = jnp.min(jnp.where(rest == top2, lane, V7X_LANES), axis=-1, keepdims=True)
        t = jnp.exp(top2 - top1)
        gate1 = 1.0 / (1.0 + t)
        gate2 = t / (1.0 + t)
        comb_ref[...] = jnp.where(lane == idx1, gate1, 0.0) + jnp.where(lane == idx2, gate2, 0.0)


def out_projection(o_sb, dil_parts, x, g_sb, g_dil, w_out, g_ffn, w_router_t=None, tm=256):
    n, d = x.shape
    d_sb = o_sb.shape[1]
    d_mix = w_out.shape[0]
    with_router = w_router_t is not None
    row = lambda width: pl.BlockSpec((tm, width), lambda i: (i, 0))
    const = lambda shape: pl.BlockSpec(shape, lambda i: (0, 0))
    in_specs = [row(d_sb)] + [row(d_mix - d_sb)] * 6 + [
        row(d), const((1, d_sb)), const((1, d_mix - d_sb)), const((d_mix, d)), const((1, d))]
    args = [o_sb]
    for o_g, lse_g in dil_parts:
        args += [o_g, lse_g]
    args += [x, g_sb, g_dil, w_out, g_ffn]
    out_specs = [row(d), row(d)]
    out_shape = [jax.ShapeDtypeStruct((n, d), F32), jax.ShapeDtypeStruct((n, d), BF16)]
    if with_router:
        in_specs.append(const(w_router_t.shape))
        args.append(w_router_t)
        out_specs.append(row(V7X_LANES))
        out_shape.append(jax.ShapeDtypeStruct((n, V7X_LANES), F32))
    return pl.pallas_call(
        functools.partial(_outproj_kernel, with_router=with_router),
        grid=(n // tm,),
        in_specs=in_specs,
        out_specs=out_specs,
        out_shape=out_shape,
        compiler_params=_params("parallel"),
    )(*args)


def _ffn_kernel(*refs, expert_width, tf):
    if expert_width:
        xn_ref, x_ref, comb_ref, wg_ref, wu_ref, wd_ref, o_ref = refs
    else:
        xn_ref, x_ref, wg_ref, wu_ref, wd_ref, o_ref = refs
    f = pl.program_id(1)

    @pl.when(f == 0)
    def _():
        o_ref[...] = x_ref[...]

    xn = xn_ref[...]
    gate = _dot(xn, wg_ref[...])
    up = _dot(xn, wu_ref[...])
    hid = gate * (1.0 / (1.0 + jnp.exp(-gate))) * up
    if expert_width:
        lane = lax.broadcasted_iota(jnp.int32, comb_ref.shape, 1)
        comb = comb_ref[...]
        e_lo = (f * tf) // expert_width
        e_hi = (f * tf + tf - 1) // expert_width
        c_lo = jnp.sum(jnp.where(lane == e_lo, comb, 0.0), axis=-1, keepdims=True)
        c_hi = jnp.sum(jnp.where(lane == e_hi, comb, 0.0), axis=-1, keepdims=True)
        hcol = lax.broadcasted_iota(jnp.int32, hid.shape, 1) + f * tf
        hid = hid * jnp.where(hcol < e_hi * expert_width, c_lo, c_hi)
    o_ref[...] += _dot(hid.astype(BF16), wd_ref[...])


def swiglu_ffn(xn, x, wg, wu, wd, comb=None, expert_width=0, tm=512, tf=512):
    n, d = x.shape
    width = wg.shape[1]
    assert width % tf == 0 and (not expert_width or tf <= expert_width)
    row = lambda width_: pl.BlockSpec((tm, width_), lambda i, f: (i, 0))
    in_specs = [row(d), row(d)] + ([row(V7X_LANES)] if expert_width else []) + [
        pl.BlockSpec((d, tf), lambda i, f: (0, f)),
        pl.BlockSpec((d, tf), lambda i, f: (0, f)),
        pl.BlockSpec((tf, d), lambda i, f: (f, 0))]
    args = [xn, x] + ([comb] if expert_width else []) + [wg, wu, wd]
    return pl.pallas_call(
        functools.partial(_ffn_kernel, expert_width=expert_width, tf=tf),
        grid=(n // tm, width // tf),
        in_specs=in_specs,
        out_specs=row(d),
        out_shape=jax.ShapeDtypeStruct((n, d), F32),
        compiler_params=_params("parallel", "arbitrary"),
    )(*args)


def _experts_side_by_side(w):
    e, d, f = w.shape
    return w.astype(BF16).transpose(1, 0, 2).reshape(d, e * f)


def _slab_gains(q_sb, k_sb, q_dil, k_dil):
    ones = jnp.ones((4, HEAD_DIM), F32)
    rep = lambda g: jnp.broadcast_to(g[None, :], (4, HEAD_DIM))
    return jnp.concatenate([rep(q_sb), rep(k_sb), ones, rep(q_dil), rep(k_dil), ones], axis=0)


def kernel(x, positions, norm_mix, w_in, q_norm_sb, k_norm_sb, q_norm_dil, k_norm_dil, out_norm_sb, out_norm_dil, w_out, norm_ffn, w_gate_dense, w_up_dense, w_down_dense, w_router, w_gate_moe, w_up_moe, w_down_moe):
    batch, seq, d = x.shape
    n = batch * seq
    depth = w_in.shape[0]
    xf = x.reshape(n, d)
    pos_b = jnp.broadcast_to(positions.reshape(n, 1), (n, HEAD_DIM))
    cos, sin = rope_tables(pos_b)
    for i in range(depth):
        gains = _slab_gains(q_norm_sb[i], k_norm_sb[i], q_norm_dil[i], k_norm_dil[i])
        qkv = in_projection(xf, norm_mix[i][None, :], w_in[i].astype(BF16), gains, cos, sin, batch)
        o_sb = stick_breaking_attention(qkv)
        dil_parts = [dilated_attention_branch(qkv, dil) for _, dil in DILATED_CONFIGS]
        j = i // 2
        moe_layer = i % 2 == 1
        outs = out_projection(
            o_sb, dil_parts, xf, out_norm_sb[i][None, :], out_norm_dil[i][None, :],
            w_out[i].astype(BF16), norm_ffn[i][None, :],
            w_router[j].T if moe_layer else None)
        if moe_layer:
            xf, xn, comb = outs
            n_exp, width, _ = w_down_moe[j].shape
            xf = swiglu_ffn(xn, xf, _experts_side_by_side(w_gate_moe[j]),
                            _experts_side_by_side(w_up_moe[j]),
                            w_down_moe[j].astype(BF16).reshape(n_exp * width, d),
                            comb=comb, expert_width=width)
        else:
            xf, xn = outs
            xf = swiglu_ffn(xn, xf, w_gate_dense[j].astype(BF16), w_up_dense[j].astype(BF16),
                            w_down_dense[j].astype(BF16))
    return xf.reshape(batch, seq, d)
```

```python
import functools
import math

import jax
import jax.numpy as jnp
from jax import lax
from jax.experimental import pallas as pl
from jax.experimental.pallas import tpu as pltpu

HEAD_DIM = 128
N_HEADS_SB = 4
N_HEADS_DIL = 4
N_SLABS = 3 * (N_HEADS_SB + N_HEADS_DIL)
SLAB_Q_SB, SLAB_K_SB, SLAB_V_SB = 0, 4, 8
SLAB_Q_DIL, SLAB_K_DIL, SLAB_V_DIL = 12, 16, 20
DILATED_CONFIGS = ((128, 1), (512, 4), (2048, 16))
DIL_STEPS = 128
N_EXPERTS = 8
ROPE_THETA = 10000.0
EPS = 1e-6
NEG_INF = -1e30
SB_ZERO_WEIGHT_LOG = 112.0

V7X_LANES = 128
V7X_VMEM_LIMIT_BYTES = 56 * 1024 * 1024

F32 = jnp.float32
BF16 = jnp.bfloat16


def _params(*semantics):
    return pltpu.CompilerParams(dimension_semantics=semantics,
                                vmem_limit_bytes=V7X_VMEM_LIMIT_BYTES)


def _dot(a, b):
    return jnp.dot(a, b, preferred_element_type=F32)


def _dot_nt(a, b):
    return lax.dot_general(a, b, (((1,), (1,)), ((), ())), preferred_element_type=F32)


def _rope_kernel(pos_ref, cos_ref, sin_ref):
    lane = lax.broadcasted_iota(jnp.int32, pos_ref.shape, 1)
    half = HEAD_DIM // 2
    pair = (lane % half).astype(F32)
    inv = jnp.exp(pair * (-2.0 * math.log(ROPE_THETA) / HEAD_DIM))
    ang = pos_ref[...].astype(F32) * inv
    cos_ref[...] = jnp.cos(ang)
    s = jnp.sin(ang)
    sin_ref[...] = jnp.where(lane < half, -s, s)


def rope_tables(pos_b, tm=1024):
    n = pos_b.shape[0]
    spec = pl.BlockSpec((tm, HEAD_DIM), lambda i: (i, 0))
    return pl.pallas_call(
        _rope_kernel,
        grid=(n // tm,),
        in_specs=[spec],
        out_specs=[spec, spec],
        out_shape=[jax.ShapeDtypeStruct((n, HEAD_DIM), F32)] * 2,
        compiler_params=_params("parallel"),
    )(pos_b)


def _inproj_kernel(x_ref, g_ref, w_ref, gains_ref, cos_ref, sin_ref, o_ref):
    x = x_ref[...]
    ms = jnp.mean(x * x, axis=-1, keepdims=True)
    xn = (x * lax.rsqrt(ms + EPS) * g_ref[...]).astype(BF16)
    cos = cos_ref[...]
    sin = sin_ref[...]
    scale = 1.0 / math.sqrt(HEAD_DIM)
    group = 4
    for j0 in range(0, N_SLABS, group):
        h = _dot(xn, w_ref[:, j0 * HEAD_DIM:(j0 + group) * HEAD_DIM])
        for j in range(j0, j0 + group):
            y = h[:, (j - j0) * HEAD_DIM:(j - j0 + 1) * HEAD_DIM]
            is_v = SLAB_V_SB <= j < SLAB_Q_DIL or j >= SLAB_V_DIL
            if not is_v:
                ms_h = jnp.mean(y * y, axis=-1, keepdims=True)
                y = y * lax.rsqrt(ms_h + EPS) * gains_ref[j:j + 1, :]
                if j >= SLAB_Q_DIL:
                    y = y * cos + pltpu.roll(y, HEAD_DIM // 2, 1) * sin
                if j < SLAB_K_SB or SLAB_Q_DIL <= j < SLAB_K_DIL:
                    y = y * scale
            o_ref[0, j] = y.astype(BF16)


def in_projection(x, g_norm, w_in, gains, cos, sin, batch, tm=512):
    n, d = x.shape
    s = n // batch
    d_in = w_in.shape[1]
    nt = s // tm
    return pl.pallas_call(
        _inproj_kernel,
        grid=(batch, nt),
        in_specs=[
            pl.BlockSpec((tm, d), lambda b, i: (b * nt + i, 0)),
            pl.BlockSpec((1, d), lambda b, i: (0, 0)),
            pl.BlockSpec((d, d_in), lambda b, i: (0, 0)),
            pl.BlockSpec((N_SLABS, HEAD_DIM), lambda b, i: (0, 0)),
            pl.BlockSpec((tm, HEAD_DIM), lambda b, i: (b * nt + i, 0)),
            pl.BlockSpec((tm, HEAD_DIM), lambda b, i: (b * nt + i, 0)),
        ],
        out_specs=pl.BlockSpec((1, N_SLABS, tm, HEAD_DIM), lambda b, i: (b, 0, i, 0)),
        out_shape=jax.ShapeDtypeStruct((batch, N_SLABS, s, HEAD_DIM), BF16),
        compiler_params=_params("parallel", "parallel"),
    )(x, g_norm, w_in, gains, cos, sin)


def _softplus(z):
    return jnp.maximum(z, 0.0) + jnp.log1p(jnp.exp(-jnp.abs(z)))


def _sb_kernel(q_ref, k_ref, v_ref, o_ref, *, blk):
    i = pl.program_id(2)
    q = q_ref[0, 0]
    row = lax.broadcasted_iota(jnp.int32, (blk, blk), 0)
    col = lax.broadcasted_iota(jnp.int32, (blk, blk), 1)
    incl = (row >= col).astype(BF16)
    strict = col < row

    def block(c, later, acc, diagonal):
        start = pl.multiple_of(c * blk, blk)
        kb = k_ref[0, 0, pl.ds(start, blk), :]
        vb = v_ref[0, 0, pl.ds(start, blk), :]
        z = _dot_nt(q, kb)
        sp = _softplus(z)
        if diagonal:
            sp = jnp.where(strict, sp, 0.0)
        sp_hi = sp.astype(BF16)
        sp_lo = (sp - sp_hi.astype(F32)).astype(BF16)
        r = _dot(sp_hi, incl) + _dot(sp_lo, incl)
        a = jnp.exp(z - r - later)
        if diagonal:
            a = jnp.where(strict, a, 0.0)
        acc = acc + _dot(a.astype(BF16), vb)
        return later + r[:, :1], acc

    later, acc = block(i, jnp.zeros((blk, 1), F32), jnp.zeros((blk, HEAD_DIM), F32), True)

    def cond(carry):
        c, min_later, _, _ = carry
        return jnp.logical_and(c >= 0, min_later < SB_ZERO_WEIGHT_LOG)

    def body(carry):
        c, _, later, acc = carry
        later, acc = block(c, later, acc, False)
        return c - 1, jnp.min(later), later, acc

    _, _, _, acc = lax.while_loop(cond, body, (i - 1, jnp.min(later), later, acc))
    o_ref[0] = acc


def stick_breaking_attention(qkv, blk=256):
    batch, _, s, dh = qkv.shape
    nh = N_HEADS_SB
    o = pl.pallas_call(
        functools.partial(_sb_kernel, blk=blk),
        grid=(batch, nh, s // blk),
        in_specs=[
            pl.BlockSpec((1, 1, blk, dh), lambda b, h, i: (b, SLAB_Q_SB + h, i, 0)),
            pl.BlockSpec((1, 1, s, dh), lambda b, h, i: (b, SLAB_K_SB + h, 0, 0)),
            pl.BlockSpec((1, 1, s, dh), lambda b, h, i: (b, SLAB_V_SB + h, 0, 0)),
        ],
        out_specs=pl.BlockSpec((1, blk, dh), lambda b, h, i: (b, i, h)),
        out_shape=jax.ShapeDtypeStruct((batch, s, nh * dh), F32),
        compiler_params=_params("parallel", "parallel", "arbitrary"),
    )(qkv, qkv, qkv)
    return o.reshape(batch * s, nh * dh)


def _dil_kernel(q_ref, k_ref, v_ref, o_ref, kf_ref, vf_ref, qf_ref, po_ref, pml_ref):
    i = pl.program_id(2)
    st = DIL_STEPS
    span, dh = qf_ref.shape
    seq = kf_ref.shape[0]
    half = dh // 2
    chunk = 512

    @pl.when(i == 0)
    def _():
        def conv(c, carry):
            rows = pl.ds(pl.multiple_of(c * chunk, chunk), chunk)
            kf_ref[rows, :] = k_ref[0, 0, rows, :].astype(F32)
            vf_ref[rows, :] = v_ref[0, 0, rows, :].astype(F32)
            return carry
        lax.fori_loop(0, seq // chunk, conv, 0)

    qf_ref[...] = q_ref[0, 0].astype(F32)

    diff = (lax.broadcasted_iota(jnp.int32, (st, 2 * st), 1)
            - lax.broadcasted_iota(jnp.int32, (st, 2 * st), 0))
    bias_normal = jnp.where((diff >= 0) & (diff <= st), 0.0, NEG_INF)
    bias_first = jnp.where(diff <= 0, 0.0, NEG_INF)
    lane = lax.broadcasted_iota(jnp.int32, (st, dh), 1)

    def strided(start, size, dil):
        return pl.ds(start, size) if dil == 1 else pl.ds(start, size, stride=dil)

    for c, (_, dil) in enumerate(DILATED_CONFIGS):
        blk_rows = st * dil

        def unit(u, carry, c=c, dil=dil, blk_rows=blk_rows):
            r = u % dil
            q_start = (u // dil) * blk_rows + r
            p0 = i * span + (u // dil) * blk_rows
            first = p0 == 0
            w_start = jnp.where(first, r, p0 - blk_rows + r)
            q = qf_ref[strided(q_start, st, dil), :].astype(BF16)
            kw = kf_ref[strided(w_start, 2 * st, dil), :].astype(BF16)
            vw = vf_ref[strided(w_start, 2 * st, dil), :].astype(BF16)
            s = _dot_nt(q, kw) + jnp.where(first, bias_first, bias_normal)
            m = jnp.max(s, axis=-1, keepdims=True)
            p = jnp.exp(s - m)
            den = jnp.sum(p, axis=-1, keepdims=True)
            po_ref[c, strided(q_start, st, dil), :] = _dot(p.astype(BF16), vw)
            pml_ref[c, strided(q_start, st, dil), :] = jnp.where(lane < half, m, den)
            return carry

        lax.fori_loop(0, span // st, unit, 0, unroll=2)

    def merge(c, carry):
        rows = pl.ds(pl.multiple_of(c * st, st), st)
        pml = [pml_ref[g, rows, :] for g in range(len(DILATED_CONFIGS))]
        ms = [t[:, :1] for t in pml]
        dens = [t[:, half:half + 1] for t in pml]
        m = functools.reduce(jnp.maximum, ms)
        ws = [jnp.exp(mg - m) for mg in ms]
        den = sum(w * dg for w, dg in zip(ws, dens))
        num = sum(w * po_ref[g, rows, :] for g, w in enumerate(ws))
        o_ref[0, rows, :] = num / den
        return carry

    lax.fori_loop(0, span // st, merge, 0, unroll=2)


def dilated_attention(qkv):
    batch, _, s, dh = qkv.shape
    nh = N_HEADS_DIL
    span = DIL_STEPS * max(dil for _, dil in DILATED_CONFIGS)
    assert s % span == 0 and s >= 2 * span
    n_cfg = len(DILATED_CONFIGS)
    o = pl.pallas_call(
        _dil_kernel,
        grid=(batch, nh, s // span),
        in_specs=[
            pl.BlockSpec((1, 1, span, dh), lambda b, h, i: (b, SLAB_Q_DIL + h, i, 0)),
            pl.BlockSpec((1, 1, s, dh), lambda b, h, i: (b, SLAB_K_DIL + h, 0, 0)),
            pl.BlockSpec((1, 1, s, dh), lambda b, h, i: (b, SLAB_V_DIL + h, 0, 0)),
        ],
        out_specs=pl.BlockSpec((1, span, dh), lambda b, h, i: (b, i, h)),
        out_shape=jax.ShapeDtypeStruct((batch, s, nh * dh), F32),
        scratch_shapes=[
            pltpu.VMEM((s, dh), F32), pltpu.VMEM((s, dh), F32), pltpu.VMEM((span, dh), F32),
            pltpu.VMEM((n_cfg, span, dh), F32), pltpu.VMEM((n_cfg, span, dh), F32)],
        compiler_params=_params("parallel", "parallel", "arbitrary"),
    )(qkv, qkv, qkv)
    return o.reshape(batch * s, nh * dh)


def _outproj_kernel(*refs, with_router):
    osb_ref, odil_ref, x_ref, gsb_ref, gdil_ref, w_ref, gffn_ref = refs[:7]
    if with_router:
        wr_ref, xo_ref, xn_ref, comb_ref = refs[7:]
    else:
        xo_ref, xn_ref = refs[7:]

    def norm(y, g):
        return y * lax.rsqrt(jnp.mean(y * y, axis=-1, keepdims=True) + EPS) * g

    d_sb = osb_ref.shape[1]
    y_sb = norm(osb_ref[...], gsb_ref[...]).astype(BF16)
    y_dil = norm(odil_ref[...], gdil_ref[...]).astype(BF16)
    x = x_ref[...] + _dot(y_sb, w_ref[:d_sb, :]) + _dot(y_dil, w_ref[d_sb:, :])
    xo_ref[...] = x
    xn = norm(x, gffn_ref[...])
    xn_ref[...] = xn.astype(BF16)

    if with_router:
        lane = lax.broadcasted_iota(jnp.int32, comb_ref.shape, 1)
        logits = jnp.full(comb_ref.shape, -jnp.inf, F32)
        for e in range(N_EXPERTS):
            le = jnp.sum(xn * wr_ref[e:e + 1, :], axis=-1, keepdims=True)
            logits = jnp.where(lane == e, le, logits)
        top1 = jnp.max(logits, axis=-1, keepdims=True)
        idx1 = jnp.min(jnp.where(logits == top1, lane, V7X_LANES), axis=-1, keepdims=True)
        rest = jnp.where(lane == idx1, -jnp.inf, logits)
        top2 = jnp.max(rest, axis=-1, keepdims=True)
        idx2 = jnp.min(jnp.where(rest == top2, lane, V7X_LANES), axis=-1, keepdims=True)
        t = jnp.exp(top2 - top1)
        gate1 = 1.0 / (1.0 + t)
        gate2 = t / (1.0 + t)
        comb_ref[...] = jnp.where(lane == idx1, gate1, 0.0) + jnp.where(lane == idx2, gate2, 0.0)


def out_projection(o_sb, o_dil, x, g_sb, g_dil, w_out, g_ffn, w_router_t=None, tm=512):
    n, d = x.shape
    d_sb = o_sb.shape[1]
    d_mix = w_out.shape[0]
    with_router = w_router_t is not None
    row = lambda width: pl.BlockSpec((tm, width), lambda i: (i, 0))
    const = lambda shape: pl.BlockSpec(shape, lambda i: (0, 0))
    in_specs = [row(d_sb), row(d_mix - d_sb), row(d), const((1, d_sb)), const((1, d_mix - d_sb)),
                const((d_mix, d)), const((1, d))]
    args = [o_sb, o_dil, x, g_sb, g_dil, w_out, g_ffn]
    out_specs = [row(d), row(d)]
    out_shape = [jax.ShapeDtypeStruct((n, d), F32), jax.ShapeDtypeStruct((n, d), BF16)]
    if with_router:
        in_specs.append(const(w_router_t.shape))
        args.append(w_router_t)
        out_specs.append(row(V7X_LANES))
        out_shape.append(jax.ShapeDtypeStruct((n, V7X_LANES), F32))
    return pl.pallas_call(
        functools.partial(_outproj_kernel, with_router=with_router),
        grid=(n // tm,),
        in_specs=in_specs,
        out_specs=out_specs,
        out_shape=out_shape,
        compiler_params=_params("parallel"),
    )(*args)


def _ffn_kernel(xn_ref, x_ref, wg_ref, wu_ref, wd_ref, o_ref):
    @pl.when(pl.program_id(1) == 0)
    def _():
        o_ref[...] = x_ref[...]

    xn = xn_ref[...]
    gate = _dot(xn, wg_ref[...])
    up = _dot(xn, wu_ref[...])
    hid = gate * (1.0 / (1.0 + jnp.exp(-gate))) * up
    o_ref[...] += _dot(hid.astype(BF16), wd_ref[...])


def swiglu_ffn(xn, x, wg, wu, wd, tm=512, tf=512):
    n, d = x.shape
    width = wg.shape[1]
    assert width % tf == 0
    row = lambda width_: pl.BlockSpec((tm, width_), lambda i, f: (i, 0))
    return pl.pallas_call(
        _ffn_kernel,
        grid=(n // tm, width // tf),
        in_specs=[row(d), row(d),
                  pl.BlockSpec((d, tf), lambda i, f: (0, f)),
                  pl.BlockSpec((d, tf), lambda i, f: (0, f)),
                  pl.BlockSpec((tf, d), lambda i, f: (f, 0))],
        out_specs=row(d),
        out_shape=jax.ShapeDtypeStruct((n, d), F32),
        compiler_params=_params("parallel", "arbitrary"),
    )(xn, x, wg, wu, wd)


MOE_CHUNK = 128


def _moe_kernel(xn_ref, x_ref, comb_ref, wg_ref, wu_ref, wd_ref, o_ref, rank_ref, gate_ref):
    e = pl.program_id(1)
    tm = xn_ref.shape[0]
    n_exp = rank_ref.shape[0]

    @pl.when(e == 0)
    def _():
        o_ref[...] = x_ref[...]
        comb_t = comb_ref[...].T
        routed = comb_t > 0.0
        before = (lax.broadcasted_iota(jnp.int32, (tm, tm), 0)
                  < lax.broadcasted_iota(jnp.int32, (tm, tm), 1))
        ranks = _dot(jnp.where(routed, 1.0, 0.0).astype(BF16),
                     jnp.where(before, 1.0, 0.0).astype(BF16))
        ranks = jnp.where(routed, ranks, -1.0)
        for ee in range(n_exp):
            rank_ref[ee] = ranks[ee:ee + 1, :]
            gate_ref[ee] = comb_t[ee:ee + 1, :]

    rank_row = rank_ref[e]
    gate_row = gate_ref[e]
    count = jnp.sum(jnp.where(rank_row >= 0.0, 1, 0))
    xn = xn_ref[...]

    def chunk(c, carry):
        slot = (c * MOE_CHUNK + lax.broadcasted_iota(jnp.int32, (MOE_CHUNK, tm), 0)).astype(F32)
        hit = rank_row == slot
        onehot = jnp.where(hit, 1.0, 0.0).astype(BF16)
        gate_slot = jnp.sum(jnp.where(hit, gate_row, 0.0), axis=1, keepdims=True)
        xs = _dot(onehot, xn).astype(BF16)
        gate = _dot(xs, wg_ref[0])
        up = _dot(xs, wu_ref[0])
        hid = gate * (1.0 / (1.0 + jnp.exp(-gate))) * up * gate_slot
        y = _dot(hid.astype(BF16), wd_ref[0]).astype(BF16)
        o_ref[...] += lax.dot_general(onehot, y, (((0,), (0,)), ((), ())),
                                      preferred_element_type=F32)
        return carry

    lax.fori_loop(0, (count + MOE_CHUNK - 1) // MOE_CHUNK, chunk, 0)


def moe_ffn(xn, x, comb, wg, wu, wd, tm=512):
    n, d = x.shape
    n_exp, _, width = wg.shape
    once = pl.Buffered(1)
    row = lambda width_, mode=None: pl.BlockSpec((tm, width_), lambda i, e: (i, 0),
                                                 pipeline_mode=mode)
    return pl.pallas_call(
        _moe_kernel,
        grid=(n // tm, n_exp),
        in_specs=[row(d, once), row(d, once), row(V7X_LANES, once),
                  pl.BlockSpec((1, d, width), lambda i, e: (e, 0, 0)),
                  pl.BlockSpec((1, d, width), lambda i, e: (e, 0, 0)),
                  pl.BlockSpec((1, width, d), lambda i, e: (e, 0, 0))],
        out_specs=row(d, once),
        out_shape=jax.ShapeDtypeStruct((n, d), F32),
        scratch_shapes=[pltpu.VMEM((n_exp, 1, tm), F32), pltpu.VMEM((n_exp, 1, tm), F32)],
        compiler_params=_params("parallel", "arbitrary"),
    )(xn, x, comb, wg, wu, wd)


def _slab_gains(q_sb, k_sb, q_dil, k_dil):
    ones = jnp.ones((4, HEAD_DIM), F32)
    rep = lambda g: jnp.broadcast_to(g[None, :], (4, HEAD_DIM))
    return jnp.concatenate([rep(q_sb), rep(k_sb), ones, rep(q_dil), rep(k_dil), ones], axis=0)


def kernel(x, positions, norm_mix, w_in, q_norm_sb, k_norm_sb, q_norm_dil, k_norm_dil, out_norm_sb, out_norm_dil, w_out, norm_ffn, w_gate_dense, w_up_dense, w_down_dense, w_router, w_gate_moe, w_up_moe, w_down_moe):
    batch, seq, d = x.shape
    n = batch * seq
    depth = w_in.shape[0]
    xf = x.reshape(n, d)
    pos_b = jnp.broadcast_to(positions.reshape(n, 1), (n, HEAD_DIM))
    cos, sin = rope_tables(pos_b)
    for i in range(depth):
        gains = _slab_gains(q_norm_sb[i], k_norm_sb[i], q_norm_dil[i], k_norm_dil[i])
        qkv = in_projection(xf, norm_mix[i][None, :], w_in[i].astype(BF16), gains, cos, sin, batch)
        o_sb = stick_breaking_attention(qkv)
        o_dil = dilated_attention(qkv)
        j = i // 2
        moe_layer = i % 2 == 1
        outs = out_projection(
            o_sb, o_dil, xf, out_norm_sb[i][None, :], out_norm_dil[i][None, :],
            w_out[i].astype(BF16), norm_ffn[i][None, :],
            w_router[j].T if moe_layer else None)
        if moe_layer:
            xf, xn, comb = outs
            xf = moe_ffn(xn, xf, comb, w_gate_moe[j].astype(BF16), w_up_moe[j].astype(BF16),
                         w_down_moe[j].astype(BF16))
        else:
            xf, xn = outs
            xf = swiglu_ffn(xn, xf, w_gate_dense[j].astype(BF16), w_up_dense[j].astype(BF16),
                            w_down_dense[j].astype(BF16))
    return xf.reshape(batch, seq, d)
```

```python
import functools
import math

import jax
import jax.numpy as jnp
from jax import lax
from jax.experimental import pallas as pl
from jax.experimental.pallas import tpu as pltpu

HEAD_DIM = 128
N_HEADS_SB = 4
N_HEADS_DIL = 4
N_SLABS = 3 * (N_HEADS_SB + N_HEADS_DIL)
SLAB_Q_SB, SLAB_K_SB, SLAB_V_SB = 0, 4, 8
SLAB_Q_DIL, SLAB_K_DIL, SLAB_V_DIL = 12, 16, 20
DILATED_CONFIGS = ((128, 1), (512, 4), (2048, 16))
DIL_STEPS = 128
N_EXPERTS = 8
ROPE_THETA = 10000.0
EPS = 1e-6
NEG_INF = -1e30
SB_ZERO_WEIGHT_LOG = 112.0

V7X_LANES = 128
V7X_VMEM_LIMIT_BYTES = 56 * 1024 * 1024

F32 = jnp.float32
BF16 = jnp.bfloat16


def _params(*semantics):
    return pltpu.CompilerParams(dimension_semantics=semantics,
                                vmem_limit_bytes=V7X_VMEM_LIMIT_BYTES)


def _dot(a, b):
    return jnp.dot(a, b, preferred_element_type=F32)


def _dot_nt(a, b):
    return lax.dot_general(a, b, (((1,), (1,)), ((), ())), preferred_element_type=F32)


def _rope_kernel(pos_ref, cos_ref, sin_ref):
    lane = lax.broadcasted_iota(jnp.int32, pos_ref.shape, 1)
    half = HEAD_DIM // 2
    pair = (lane % half).astype(F32)
    inv = jnp.exp(pair * (-2.0 * math.log(ROPE_THETA) / HEAD_DIM))
    ang = pos_ref[...].astype(F32) * inv
    cos_ref[...] = jnp.cos(ang)
    s = jnp.sin(ang)
    sin_ref[...] = jnp.where(lane < half, -s, s)


def rope_tables(pos_b, tm=1024):
    n = pos_b.shape[0]
    spec = pl.BlockSpec((tm, HEAD_DIM), lambda i: (i, 0))
    return pl.pallas_call(
        _rope_kernel,
        grid=(n // tm,),
        in_specs=[spec],
        out_specs=[spec, spec],
        out_shape=[jax.ShapeDtypeStruct((n, HEAD_DIM), F32)] * 2,
        compiler_params=_params("parallel"),
    )(pos_b)


def _inproj_kernel(x_ref, g_ref, w_ref, gains_ref, cos_ref, sin_ref, o_ref):
    x = x_ref[...]
    ms = jnp.mean(x * x, axis=-1, keepdims=True)
    xn = (x * lax.rsqrt(ms + EPS) * g_ref[...]).astype(BF16)
    cos = cos_ref[...]
    sin = sin_ref[...]
    scale = 1.0 / math.sqrt(HEAD_DIM)
    group = 4
    for j0 in range(0, N_SLABS, group):
        h = _dot(xn, w_ref[:, j0 * HEAD_DIM:(j0 + group) * HEAD_DIM])
        for j in range(j0, j0 + group):
            y = h[:, (j - j0) * HEAD_DIM:(j - j0 + 1) * HEAD_DIM]
            is_v = SLAB_V_SB <= j < SLAB_Q_DIL or j >= SLAB_V_DIL
            if not is_v:
                ms_h = jnp.mean(y * y, axis=-1, keepdims=True)
                y = y * lax.rsqrt(ms_h + EPS) * gains_ref[j:j + 1, :]
                if j >= SLAB_Q_DIL:
                    y = y * cos + pltpu.roll(y, HEAD_DIM // 2, 1) * sin
                if j < SLAB_K_SB or SLAB_Q_DIL <= j < SLAB_K_DIL:
                    y = y * scale
            o_ref[0, j] = y.astype(BF16)


def in_projection(x, g_norm, w_in, gains, cos, sin, batch, tm=512):
    n, d = x.shape
    s = n // batch
    d_in = w_in.shape[1]
    nt = s // tm
    return pl.pallas_call(
        _inproj_kernel,
        grid=(batch, nt),
        in_specs=[
            pl.BlockSpec((tm, d), lambda b, i: (b * nt + i, 0)),
            pl.BlockSpec((1, d), lambda b, i: (0, 0)),
            pl.BlockSpec((d, d_in), lambda b, i: (0, 0)),
            pl.BlockSpec((N_SLABS, HEAD_DIM), lambda b, i: (0, 0)),
            pl.BlockSpec((tm, HEAD_DIM), lambda b, i: (b * nt + i, 0)),
            pl.BlockSpec((tm, HEAD_DIM), lambda b, i: (b * nt + i, 0)),
        ],
        out_specs=pl.BlockSpec((1, N_SLABS, tm, HEAD_DIM), lambda b, i: (b, 0, i, 0)),
        out_shape=jax.ShapeDtypeStruct((batch, N_SLABS, s, HEAD_DIM), BF16),
        compiler_params=_params("parallel", "parallel"),
    )(x, g_norm, w_in, gains, cos, sin)


def _softplus(z):
    return jnp.maximum(z, 0.0) + jnp.log(1.0 + jnp.exp(-jnp.abs(z)))


def _sb_kernel(q_ref, k_ref, v_ref, o_ref, *, blk, nsub):
    i = pl.program_id(2)
    row = lax.broadcasted_iota(jnp.int32, (blk, blk), 0)
    col = lax.broadcasted_iota(jnp.int32, (blk, blk), 1)
    incl = (row >= col).astype(BF16)
    strict = col < row

    def block(q, c, later, acc, diagonal):
        start = pl.multiple_of(jnp.maximum(c, 0) * blk, blk)
        kb = k_ref[0, 0, pl.ds(start, blk), :]
        vb = v_ref[0, 0, pl.ds(start, blk), :]
        z = _dot_nt(q, kb)
        sp = _softplus(z)
        if diagonal:
            sp = jnp.where(strict, sp, 0.0)
        sp_hi = sp.astype(BF16)
        sp_lo = (sp - sp_hi.astype(F32)).astype(BF16)
        r = _dot(sp_hi, incl) + _dot(sp_lo, incl)
        a = jnp.exp(z - r - later)
        total = r[:, :1]
        if diagonal:
            a = jnp.where(strict, a, 0.0)
        else:
            a = jnp.where(c >= 0, a, 0.0)
            total = jnp.where(c >= 0, total, 0.0)
        return later + total, acc + _dot(a.astype(BF16), vb)

    qs = [q_ref[0, 0, j * blk:(j + 1) * blk, :] for j in range(nsub)]
    own = [i * nsub + j for j in range(nsub)]
    state = [block(qs[j], own[j], jnp.zeros((blk, 1), F32), jnp.zeros((blk, HEAD_DIM), F32), True)
             for j in range(nsub)]

    def unfinished(back, laters):
        flags = [jnp.logical_and(own[j] - back >= 0, jnp.min(laters[j]) < SB_ZERO_WEIGHT_LOG)
                 for j in range(nsub)]
        return functools.reduce(jnp.logical_or, flags)

    def cond(carry):
        return carry[1]

    def body(carry):
        back, _, laters, accs = carry
        new = [block(qs[j], own[j] - back, laters[j], accs[j], False) for j in range(nsub)]
        laters, accs = [s[0] for s in new], [s[1] for s in new]
        return back + 1, unfinished(back + 1, laters), laters, accs

    laters, accs = [s[0] for s in state], [s[1] for s in state]
    _, _, _, accs = lax.while_loop(cond, body, (1, unfinished(1, laters), laters, accs))
    for j in range(nsub):
        o_ref[0, j * blk:(j + 1) * blk, :] = accs[j]


def stick_breaking_attention(qkv, blk=256, nsub=2):
    batch, _, s, dh = qkv.shape
    nh = N_HEADS_SB
    o = pl.pallas_call(
        functools.partial(_sb_kernel, blk=blk, nsub=nsub),
        grid=(batch, nh, s // (blk * nsub)),
        in_specs=[
            pl.BlockSpec((1, 1, blk * nsub, dh), lambda b, h, i: (b, SLAB_Q_SB + h, i, 0)),
            pl.BlockSpec((1, 1, s, dh), lambda b, h, i: (b, SLAB_K_SB + h, 0, 0)),
            pl.BlockSpec((1, 1, s, dh), lambda b, h, i: (b, SLAB_V_SB + h, 0, 0)),
        ],
        out_specs=pl.BlockSpec((1, blk * nsub, dh), lambda b, h, i: (b, i, h)),
        out_shape=jax.ShapeDtypeStruct((batch, s, nh * dh), F32),
        compiler_params=_params("parallel", "parallel", "arbitrary"),
    )(qkv, qkv, qkv)
    return o.reshape(batch * s, nh * dh)


def _dil_kernel(q_ref, k_ref, v_ref, o_ref, kf_ref, vf_ref, qf_ref, po_ref, pm_ref, pl_ref):
    i = pl.program_id(2)
    st = DIL_STEPS
    span, dh = qf_ref.shape
    seq = kf_ref.shape[0]
    chunk = 512

    @pl.when(i == 0)
    def _():
        def conv(c, carry):
            rows = pl.ds(pl.multiple_of(c * chunk, chunk), chunk)
            kf_ref[rows, :] = k_ref[0, 0, rows, :].astype(F32)
            vf_ref[rows, :] = v_ref[0, 0, rows, :].astype(F32)
            return carry
        lax.fori_loop(0, seq // chunk, conv, 0)

    qf_ref[...] = q_ref[0, 0].astype(F32)

    diff = (lax.broadcasted_iota(jnp.int32, (st, 2 * st), 1)
            - lax.broadcasted_iota(jnp.int32, (st, 2 * st), 0))
    bias_normal = jnp.where((diff >= 0) & (diff <= st), 0.0, NEG_INF)
    bias_first = jnp.where(diff <= 0, 0.0, NEG_INF)
    ones = jnp.ones((2 * st, dh), BF16)

    def strided(start, size, dil):
        return pl.ds(start, size) if dil == 1 else pl.ds(start, size, stride=dil)

    for c, (_, dil) in enumerate(DILATED_CONFIGS):
        blk_rows = st * dil

        def unit(u, carry, c=c, dil=dil, blk_rows=blk_rows):
            r = u % dil
            q_start = (u // dil) * blk_rows + r
            p0 = i * span + (u // dil) * blk_rows
            first = p0 == 0
            w_start = jnp.where(first, r, p0 - blk_rows + r)
            q = qf_ref[strided(q_start, st, dil), :].astype(BF16)
            kw = kf_ref[strided(w_start, 2 * st, dil), :].astype(BF16)
            vw = vf_ref[strided(w_start, 2 * st, dil), :].astype(BF16)
            s = _dot_nt(q, kw) + jnp.where(first, bias_first, bias_normal)
            m = jnp.max(s, axis=-1, keepdims=True)
            p = jnp.exp(s - m).astype(BF16)
            o_den = _dot(p, jnp.concatenate([vw, ones], axis=1))
            rows = strided(q_start, st, dil)
            po_ref[c, rows, :] = o_den[:, :dh]
            pl_ref[c, rows, :] = o_den[:, dh:]
            pm_ref[c, rows, :] = jnp.broadcast_to(m, (st, dh))
            return carry

        lax.fori_loop(0, span // st, unit, 0, unroll=8)

    def merge(c, carry):
        rows = pl.ds(pl.multiple_of(c * st, st), st)
        ms = [pm_ref[g, rows, :] for g in range(len(DILATED_CONFIGS))]
        m = functools.reduce(jnp.maximum, ms)
        ws = [jnp.exp(mg - m) for mg in ms]
        den = sum(w * pl_ref[g, rows, :] for g, w in enumerate(ws))
        num = sum(w * po_ref[g, rows, :] for g, w in enumerate(ws))
        o_ref[0, rows, :] = num / den
        return carry

    lax.fori_loop(0, span // st, merge, 0, unroll=2)


def dilated_attention(qkv):
    batch, _, s, dh = qkv.shape
    nh = N_HEADS_DIL
    span = DIL_STEPS * max(dil for _, dil in DILATED_CONFIGS)
    assert s % span == 0 and s >= 2 * span
    n_cfg = len(DILATED_CONFIGS)
    o = pl.pallas_call(
        _dil_kernel,
        grid=(batch, nh, s // span),
        in_specs=[
            pl.BlockSpec((1, 1, span, dh), lambda b, h, i: (b, SLAB_Q_DIL + h, i, 0)),
            pl.BlockSpec((1, 1, s, dh), lambda b, h, i: (b, SLAB_K_DIL + h, 0, 0)),
            pl.BlockSpec((1, 1, s, dh), lambda b, h, i: (b, SLAB_V_DIL + h, 0, 0)),
        ],
        out_specs=pl.BlockSpec((1, span, dh), lambda b, h, i: (b, i, h)),
        out_shape=jax.ShapeDtypeStruct((batch, s, nh * dh), F32),
        scratch_shapes=[
            pltpu.VMEM((s, dh), F32), pltpu.VMEM((s, dh), F32), pltpu.VMEM((span, dh), F32),
            pltpu.VMEM((n_cfg, span, dh), F32), pltpu.VMEM((n_cfg, span, dh), F32),
            pltpu.VMEM((n_cfg, span, dh), F32)],
        compiler_params=_params("parallel", "parallel", "arbitrary"),
    )(qkv, qkv, qkv)
    return o.reshape(batch * s, nh * dh)


def _outproj_kernel(*refs, with_router):
    osb_ref, odil_ref, x_ref, gsb_ref, gdil_ref, w_ref, gffn_ref = refs[:7]
    if with_router:
        wr_ref, xo_ref, xn_ref, comb_ref = refs[7:]
    else:
        xo_ref, xn_ref = refs[7:]

    def norm(y, g):
        return y * lax.rsqrt(jnp.mean(y * y, axis=-1, keepdims=True) + EPS) * g

    d_sb = osb_ref.shape[1]
    y_sb = norm(osb_ref[...], gsb_ref[...]).astype(BF16)
    y_dil = norm(odil_ref[...], gdil_ref[...]).astype(BF16)
    x = x_ref[...] + _dot(y_sb, w_ref[:d_sb, :]) + _dot(y_dil, w_ref[d_sb:, :])
    xo_ref[...] = x
    xn = norm(x, gffn_ref[...])
    xn_ref[...] = xn.astype(BF16)

    if with_router:
        lane = lax.broadcasted_iota(jnp.int32, comb_ref.shape, 1)
        logits = jnp.full(comb_ref.shape, -jnp.inf, F32)
        for e in range(N_EXPERTS):
            le = jnp.sum(xn * wr_ref[e:e + 1, :], axis=-1, keepdims=True)
            logits = jnp.where(lane == e, le, logits)
        top1 = jnp.max(logits, axis=-1, keepdims=True)
        idx1 = jnp.min(jnp.where(logits == top1, lane, V7X_LANES), axis=-1, keepdims=True)
        rest = jnp.where(lane == idx1, -jnp.inf, logits)
        top2 = jnp.max(rest, axis=-1, keepdims=True)
        idx2 = jnp.min(jnp.where(rest == top2, lane, V7X_LANES), axis=-1, keepdims=True)
        t = jnp.exp(top2 - top1)
        gate1 = 1.0 / (1.0 + t)
        gate2 = t / (1.0 + t)
        comb_ref[...] = jnp.where(lane == idx1, gate1, 0.0) + jnp.where(lane == idx2, gate2, 0.0)


def out_projection(o_sb, o_dil, x, g_sb, g_dil, w_out, g_ffn, w_router_t=None, tm=512):
    n, d = x.shape
    d_sb = o_sb.shape[1]
    d_mix = w_out.shape[0]
    with_router = w_router_t is not None
    row = lambda width: pl.BlockSpec((tm, width), lambda i: (i, 0))
    const = lambda shape: pl.BlockSpec(shape, lambda i: (0, 0))
    in_specs = [row(d_sb), row(d_mix - d_sb), row(d), const((1, d_sb)), const((1, d_mix - d_sb)),
                const((d_mix, d)), const((1, d))]
    args = [o_sb, o_dil, x, g_sb, g_dil, w_out, g_ffn]
    out_specs = [row(d), row(d)]
    out_shape = [jax.ShapeDtypeStruct((n, d), F32), jax.ShapeDtypeStruct((n, d), BF16)]
    if with_router:
        in_specs.append(const(w_router_t.shape))
        args.append(w_router_t)
        out_specs.append(row(V7X_LANES))
        out_shape.append(jax.ShapeDtypeStruct((n, V7X_LANES), F32))
    return pl.pallas_call(
        functools.partial(_outproj_kernel, with_router=with_router),
        grid=(n // tm,),
        in_specs=in_specs,
        out_specs=out_specs,
        out_shape=out_shape,
        compiler_params=_params("parallel"),
    )(*args)


def _ffn_kernel(xn_ref, x_ref, wg_ref, wu_ref, wd_ref, o_ref):
    @pl.when(pl.program_id(1) == 0)
    def _():
        o_ref[...] = x_ref[...]

    xn = xn_ref[...]
    gate = _dot(xn, wg_ref[...])
    up = _dot(xn, wu_ref[...])
    hid = gate * (1.0 / (1.0 + jnp.exp(-gate))) * up
    o_ref[...] += _dot(hid.astype(BF16), wd_ref[...])


def swiglu_ffn(xn, x, wg, wu, wd, tm=512, tf=512):
    n, d = x.shape
    width = wg.shape[1]
    assert width % tf == 0
    row = lambda width_: pl.BlockSpec((tm, width_), lambda i, f: (i, 0))
    return pl.pallas_call(
        _ffn_kernel,
        grid=(n // tm, width // tf),
        in_specs=[row(d), row(d),
                  pl.BlockSpec((d, tf), lambda i, f: (0, f)),
                  pl.BlockSpec((d, tf), lambda i, f: (0, f)),
                  pl.BlockSpec((tf, d), lambda i, f: (f, 0))],
        out_specs=row(d),
        out_shape=jax.ShapeDtypeStruct((n, d), F32),
        compiler_params=_params("parallel", "arbitrary"),
    )(xn, x, wg, wu, wd)


MOE_CHUNK = 128


def _moe_kernel(xn_ref, x_ref, comb_ref, wg_ref, wu_ref, wd_ref, o_ref, rank_ref, gate_ref):
    e = pl.program_id(1)
    tm = xn_ref.shape[0]
    n_exp = rank_ref.shape[0]

    @pl.when(e == 0)
    def _():
        o_ref[...] = x_ref[...]
        comb_t = comb_ref[...].T
        routed = comb_t > 0.0
        before = (lax.broadcasted_iota(jnp.int32, (tm, tm), 0)
                  < lax.broadcasted_iota(jnp.int32, (tm, tm), 1))
        ranks = _dot(jnp.where(routed, 1.0, 0.0).astype(BF16),
                     jnp.where(before, 1.0, 0.0).astype(BF16))
        ranks = jnp.where(routed, ranks, -1.0)
        for ee in range(n_exp):
            rank_ref[ee] = ranks[ee:ee + 1, :]
            gate_ref[ee] = comb_t[ee:ee + 1, :]

    rank_row = rank_ref[e]
    gate_row = gate_ref[e]
    count = jnp.sum(jnp.where(rank_row >= 0.0, 1, 0))
    xn = xn_ref[...]

    def chunk(c, carry):
        slot = (c * MOE_CHUNK + lax.broadcasted_iota(jnp.int32, (MOE_CHUNK, tm), 0)).astype(F32)
        hit = rank_row == slot
        onehot = jnp.where(hit, 1.0, 0.0).astype(BF16)
        gate_slot = jnp.sum(jnp.where(hit, gate_row, 0.0), axis=1, keepdims=True)
        xs = _dot(onehot, xn).astype(BF16)
        gate = _dot(xs, wg_ref[0])
        up = _dot(xs, wu_ref[0])
        hid = gate * (1.0 / (1.0 + jnp.exp(-gate))) * up * gate_slot
        y = _dot(hid.astype(BF16), wd_ref[0]).astype(BF16)
        o_ref[...] += lax.dot_general(onehot, y, (((0,), (0,)), ((), ())),
                                      preferred_element_type=F32)
        return carry

    lax.fori_loop(0, (count + MOE_CHUNK - 1) // MOE_CHUNK, chunk, 0)


def moe_ffn(xn, x, comb, wg, wu, wd, tm=512):
    n, d = x.shape
    n_exp, _, width = wg.shape
    once = pl.Buffered(1)
    row = lambda width_, mode=None: pl.BlockSpec((tm, width_), lambda i, e: (i, 0),
                                                 pipeline_mode=mode)
    return pl.pallas_call(
        _moe_kernel,
        grid=(n // tm, n_exp),
        in_specs=[row(d, once), row(d, once), row(V7X_LANES, once),
                  pl.BlockSpec((1, d, width), lambda i, e: (e, 0, 0)),
                  pl.BlockSpec((1, d, width), lambda i, e: (e, 0, 0)),
                  pl.BlockSpec((1, width, d), lambda i, e: (e, 0, 0))],
        out_specs=row(d, once),
        out_shape=jax.ShapeDtypeStruct((n, d), F32),
        scratch_shapes=[pltpu.VMEM((n_exp, 1, tm), F32), pltpu.VMEM((n_exp, 1, tm), F32)],
        compiler_params=_params("parallel", "arbitrary"),
    )(xn, x, comb, wg, wu, wd)


def _slab_gains(q_sb, k_sb, q_dil, k_dil):
    ones = jnp.ones((4, HEAD_DIM), F32)
    rep = lambda g: jnp.broadcast_to(g[None, :], (4, HEAD_DIM))
    return jnp.concatenate([rep(q_sb), rep(k_sb), ones, rep(q_dil), rep(k_dil), ones], axis=0)


def kernel(x, positions, norm_mix, w_in, q_norm_sb, k_norm_sb, q_norm_dil, k_norm_dil, out_norm_sb, out_norm_dil, w_out, norm_ffn, w_gate_dense, w_up_dense, w_down_dense, w_router, w_gate_moe, w_up_moe, w_down_moe):
    batch, seq, d = x.shape
    n = batch * seq
    depth = w_in.shape[0]
    xf = x.reshape(n, d)
    pos_b = jnp.broadcast_to(positions.reshape(n, 1), (n, HEAD_DIM))
    cos, sin = rope_tables(pos_b)
    for i in range(depth):
        gains = _slab_gains(q_norm_sb[i], k_norm_sb[i], q_norm_dil[i], k_norm_dil[i])
        qkv = in_projection(xf, norm_mix[i][None, :], w_in[i].astype(BF16), gains, cos, sin, batch)
        o_sb = stick_breaking_attention(qkv)
        o_dil = dilated_attention(qkv)
        j = i // 2
        moe_layer = i % 2 == 1
        outs = out_projection(
            o_sb, o_dil, xf, out_norm_sb[i][None, :], out_norm_dil[i][None, :],
            w_out[i].astype(BF16), norm_ffn[i][None, :],
            w_router[j].T if moe_layer else None)
        if moe_layer:
            xf, xn, comb = outs
            xf = moe_ffn(xn, xf, comb, w_gate_moe[j].astype(BF16), w_up_moe[j].astype(BF16),
                         w_down_moe[j].astype(BF16))
        else:
            xf, xn = outs
            xf = swiglu_ffn(xn, xf, w_gate_dense[j].astype(BF16), w_up_dense[j].astype(BF16),
                            w_down_dense[j].astype(BF16))
    return xf.reshape(batch, seq, d)
```

```python
import functools
import math

import jax
import jax.numpy as jnp
from jax import lax
from jax.experimental import pallas as pl
from jax.experimental.pallas import tpu as pltpu

HEAD_DIM = 128
N_HEADS_SB = 4
N_HEADS_DIL = 4
N_SLABS = 3 * (N_HEADS_SB + N_HEADS_DIL)
SLAB_Q_SB, SLAB_K_SB, SLAB_V_SB = 0, 4, 8
SLAB_Q_DIL, SLAB_K_DIL, SLAB_V_DIL = 12, 16, 20
DILATED_CONFIGS = ((128, 1), (512, 4), (2048, 16))
DIL_STEPS = 128
N_EXPERTS = 8
ROPE_THETA = 10000.0
EPS = 1e-6
NEG_INF = -1e30
SB_ZERO_WEIGHT_LOG = 112.0

V7X_LANES = 128
V7X_SUBLANES = 8
V7X_BF16_ROWS = 16
V7X_VMEM_LIMIT_BYTES = 56 * 1024 * 1024

F32 = jnp.float32
BF16 = jnp.bfloat16


def _params(*semantics):
    return pltpu.CompilerParams(dimension_semantics=semantics,
                                vmem_limit_bytes=V7X_VMEM_LIMIT_BYTES)


def _dot(a, b):
    return jnp.dot(a, b, preferred_element_type=F32)


def _dot_nt(a, b):
    return lax.dot_general(a, b, (((1,), (1,)), ((), ())), preferred_element_type=F32)


def _rope_kernel(pos_ref, cos_ref, sin_ref):
    lane = lax.broadcasted_iota(jnp.int32, pos_ref.shape, 1)
    half = HEAD_DIM // 2
    pair = (lane % half).astype(F32)
    inv = jnp.exp(pair * (-2.0 * math.log(ROPE_THETA) / HEAD_DIM))
    ang = pos_ref[...].astype(F32) * inv
    cos_ref[...] = jnp.cos(ang)
    s = jnp.sin(ang)
    sin_ref[...] = jnp.where(lane < half, -s, s)


def rope_tables(pos_b, tm=1024):
    n = pos_b.shape[0]
    spec = pl.BlockSpec((tm, HEAD_DIM), lambda i: (i, 0))
    return pl.pallas_call(
        _rope_kernel,
        grid=(n // tm,),
        in_specs=[spec],
        out_specs=[spec, spec],
        out_shape=[jax.ShapeDtypeStruct((n, HEAD_DIM), F32)] * 2,
        compiler_params=_params("parallel"),
    )(pos_b)


def _inproj_kernel(x_ref, g_ref, w_ref, gains_ref, cos_ref, sin_ref, o_ref):
    x = x_ref[...]
    ms = jnp.mean(x * x, axis=-1, keepdims=True)
    xn = (x * lax.rsqrt(ms + EPS) * g_ref[...]).astype(BF16)
    cos = cos_ref[...]
    sin = sin_ref[...]
    scale = 1.0 / math.sqrt(HEAD_DIM)
    group = 4
    for j0 in range(0, N_SLABS, group):
        h = _dot(xn, w_ref[:, j0 * HEAD_DIM:(j0 + group) * HEAD_DIM])
        for j in range(j0, j0 + group):
            y = h[:, (j - j0) * HEAD_DIM:(j - j0 + 1) * HEAD_DIM]
            is_v = SLAB_V_SB <= j < SLAB_Q_DIL or j >= SLAB_V_DIL
            if not is_v:
                ms_h = jnp.mean(y * y, axis=-1, keepdims=True)
                y = y * lax.rsqrt(ms_h + EPS) * gains_ref[j:j + 1, :]
                if j >= SLAB_Q_DIL:
                    y = y * cos + pltpu.roll(y, HEAD_DIM // 2, 1) * sin
                if j < SLAB_K_SB or SLAB_Q_DIL <= j < SLAB_K_DIL:
                    y = y * scale
            o_ref[0, j] = y.astype(BF16)


def in_projection(x, g_norm, w_in, gains, cos, sin, batch, tm=512):
    n, d = x.shape
    s = n // batch
    d_in = w_in.shape[1]
    nt = s // tm
    return pl.pallas_call(
        _inproj_kernel,
        grid=(batch, nt),
        in_specs=[
            pl.BlockSpec((tm, d), lambda b, i: (b * nt + i, 0)),
            pl.BlockSpec((1, d), lambda b, i: (0, 0)),
            pl.BlockSpec((d, d_in), lambda b, i: (0, 0)),
            pl.BlockSpec((N_SLABS, HEAD_DIM), lambda b, i: (0, 0)),
            pl.BlockSpec((tm, HEAD_DIM), lambda b, i: (b * nt + i, 0)),
            pl.BlockSpec((tm, HEAD_DIM), lambda b, i: (b * nt + i, 0)),
        ],
        out_specs=pl.BlockSpec((1, N_SLABS, tm, HEAD_DIM), lambda b, i: (b, 0, i, 0)),
        out_shape=jax.ShapeDtypeStruct((batch, N_SLABS, s, HEAD_DIM), BF16),
        compiler_params=_params("parallel", "parallel"),
    )(x, g_norm, w_in, gains, cos, sin)


def _softplus(z):
    return jnp.maximum(z, 0.0) + jnp.log(1.0 + jnp.exp(-jnp.abs(z)))


def _sb_kernel(q_ref, k_ref, v_ref, o_ref, *, blk, nsub):
    i = pl.program_id(2)
    row = lax.broadcasted_iota(jnp.int32, (blk, blk), 0)
    col = lax.broadcasted_iota(jnp.int32, (blk, blk), 1)
    incl = (row >= col).astype(BF16)
    strict = col < row

    def block(q, c, later, acc, diagonal):
        start = pl.multiple_of(jnp.maximum(c, 0) * blk, blk)
        kb = k_ref[0, 0, pl.ds(start, blk), :]
        vb = v_ref[0, 0, pl.ds(start, blk), :]
        z = _dot_nt(q, kb)
        sp = _softplus(z)
        if diagonal:
            sp = jnp.where(strict, sp, 0.0)
        sp_hi = sp.astype(BF16)
        sp_lo = (sp - sp_hi.astype(F32)).astype(BF16)
        r = _dot(sp_hi, incl) + _dot(sp_lo, incl)
        a = jnp.exp(z - r - later)
        total = r[:, :1]
        if diagonal:
            a = jnp.where(strict, a, 0.0)
        else:
            a = jnp.where(c >= 0, a, 0.0)
            total = jnp.where(c >= 0, total, 0.0)
        return later + total, acc + _dot(a.astype(BF16), vb)

    qs = [q_ref[0, 0, j * blk:(j + 1) * blk, :] for j in range(nsub)]
    own = [i * nsub + j for j in range(nsub)]
    state = [block(qs[j], own[j], jnp.zeros((blk, 1), F32), jnp.zeros((blk, HEAD_DIM), F32), True)
             for j in range(nsub)]

    def unfinished(back, laters):
        flags = [jnp.logical_and(own[j] - back >= 0, jnp.min(laters[j]) < SB_ZERO_WEIGHT_LOG)
                 for j in range(nsub)]
        return functools.reduce(jnp.logical_or, flags)

    def cond(carry):
        return carry[1]

    def body(carry):
        back, _, laters, accs = carry
        new = [block(qs[j], own[j] - back, laters[j], accs[j], False) for j in range(nsub)]
        laters, accs = [s[0] for s in new], [s[1] for s in new]
        return back + 1, unfinished(back + 1, laters), laters, accs

    laters, accs = [s[0] for s in state], [s[1] for s in state]
    _, _, _, accs = lax.while_loop(cond, body, (1, unfinished(1, laters), laters, accs))
    for j in range(nsub):
        o_ref[0, j * blk:(j + 1) * blk, :] = accs[j]


def stick_breaking_attention(qkv, blk=256, nsub=2):
    batch, _, s, dh = qkv.shape
    nh = N_HEADS_SB
    o = pl.pallas_call(
        functools.partial(_sb_kernel, blk=blk, nsub=nsub),
        grid=(batch, nh, s // (blk * nsub)),
        in_specs=[
            pl.BlockSpec((1, 1, blk * nsub, dh), lambda b, h, i: (b, SLAB_Q_SB + h, i, 0)),
            pl.BlockSpec((1, 1, s, dh), lambda b, h, i: (b, SLAB_K_SB + h, 0, 0)),
            pl.BlockSpec((1, 1, s, dh), lambda b, h, i: (b, SLAB_V_SB + h, 0, 0)),
        ],
        out_specs=pl.BlockSpec((1, blk * nsub, dh), lambda b, h, i: (b, i, h)),
        out_shape=jax.ShapeDtypeStruct((batch, s, nh * dh), F32),
        compiler_params=_params("parallel", "parallel", "arbitrary"),
    )(qkv, qkv, qkv)
    return o.reshape(batch * s, nh * dh)


def _dil_kernel(q_ref, k_ref, v_ref, o_ref, kf_ref, vf_ref, qf_ref, po_ref, pm_ref, pl_ref):
    i = pl.program_id(2)
    st = DIL_STEPS
    span, dh = qf_ref.shape
    seq = kf_ref.shape[0]
    chunk = 512

    @pl.when(i == 0)
    def _():
        def conv(c, carry):
            rows = pl.ds(pl.multiple_of(c * chunk, chunk), chunk)
            kf_ref[rows, :] = k_ref[0, 0, rows, :].astype(F32)
            vf_ref[rows, :] = v_ref[0, 0, rows, :].astype(F32)
            return carry
        lax.fori_loop(0, seq // chunk, conv, 0)

    qf_ref[...] = q_ref[0, 0].astype(F32)

    diff = (lax.broadcasted_iota(jnp.int32, (st, 2 * st), 1)
            - lax.broadcasted_iota(jnp.int32, (st, 2 * st), 0))
    bias_normal = jnp.where((diff >= 0) & (diff <= st), 0.0, NEG_INF)
    bias_first = jnp.where(diff <= 0, 0.0, NEG_INF)
    ones = jnp.ones((2 * st, dh), BF16)

    def strided(start, size, dil):
        return pl.ds(start, size) if dil == 1 else pl.ds(start, size, stride=dil)

    for c, (_, dil) in enumerate(DILATED_CONFIGS):
        blk_rows = st * dil

        def unit(u, carry, c=c, dil=dil, blk_rows=blk_rows):
            r = u % dil
            q_start = (u // dil) * blk_rows + r
            p0 = i * span + (u // dil) * blk_rows
            first = p0 == 0
            w_start = jnp.where(first, r, p0 - blk_rows + r)
            q = qf_ref[strided(q_start, st, dil), :].astype(BF16)
            kw = kf_ref[strided(w_start, 2 * st, dil), :].astype(BF16)
            vw = vf_ref[strided(w_start, 2 * st, dil), :].astype(BF16)
            s = _dot_nt(q, kw) + jnp.where(first, bias_first, bias_normal)
            m = jnp.max(s, axis=-1, keepdims=True)
            p = jnp.exp(s - m).astype(BF16)
            o_den = _dot(p, jnp.concatenate([vw, ones], axis=1))
            rows = strided(q_start, st, dil)
            po_ref[c, rows, :] = o_den[:, :dh]
            pl_ref[c, rows, :] = o_den[:, dh:]
            pm_ref[c, rows, :] = jnp.broadcast_to(m, (st, dh))
            return carry

        lax.fori_loop(0, span // st, unit, 0, unroll=8)

    def merge(c, carry):
        rows = pl.ds(pl.multiple_of(c * st, st), st)
        ms = [pm_ref[g, rows, :] for g in range(len(DILATED_CONFIGS))]
        m = functools.reduce(jnp.maximum, ms)
        ws = [jnp.exp(mg - m) for mg in ms]
        den = sum(w * pl_ref[g, rows, :] for g, w in enumerate(ws))
        num = sum(w * po_ref[g, rows, :] for g, w in enumerate(ws))
        o_ref[0, rows, :] = num / den
        return carry

    lax.fori_loop(0, span // st, merge, 0, unroll=2)


def dilated_attention(qkv):
    batch, _, s, dh = qkv.shape
    nh = N_HEADS_DIL
    span = DIL_STEPS * max(dil for _, dil in DILATED_CONFIGS)
    assert s % span == 0 and s >= 2 * span
    n_cfg = len(DILATED_CONFIGS)
    o = pl.pallas_call(
        _dil_kernel,
        grid=(batch, nh, s // span),
        in_specs=[
            pl.BlockSpec((1, 1, span, dh), lambda b, h, i: (b, SLAB_Q_DIL + h, i, 0)),
            pl.BlockSpec((1, 1, s, dh), lambda b, h, i: (b, SLAB_K_DIL + h, 0, 0)),
            pl.BlockSpec((1, 1, s, dh), lambda b, h, i: (b, SLAB_V_DIL + h, 0, 0)),
        ],
        out_specs=pl.BlockSpec((1, span, dh), lambda b, h, i: (b, i, h)),
        out_shape=jax.ShapeDtypeStruct((batch, s, nh * dh), F32),
        scratch_shapes=[
            pltpu.VMEM((s, dh), F32), pltpu.VMEM((s, dh), F32), pltpu.VMEM((span, dh), F32),
            pltpu.VMEM((n_cfg, span, dh), F32), pltpu.VMEM((n_cfg, span, dh), F32),
            pltpu.VMEM((n_cfg, span, dh), F32)],
        compiler_params=_params("parallel", "parallel", "arbitrary"),
    )(qkv, qkv, qkv)
    return o.reshape(batch * s, nh * dh)


def _outproj_kernel(*refs, with_router):
    osb_ref, odil_ref, x_ref, gsb_ref, gdil_ref, w_ref, gffn_ref = refs[:7]
    if with_router:
        wr_ref, xo_ref, xn_ref, comb_ref, cnt_ref = refs[7:]
    else:
        xo_ref, xn_ref = refs[7:]

    def norm(y, g):
        return y * lax.rsqrt(jnp.mean(y * y, axis=-1, keepdims=True) + EPS) * g

    d_sb = osb_ref.shape[1]
    y_sb = norm(osb_ref[...], gsb_ref[...]).astype(BF16)
    y_dil = norm(odil_ref[...], gdil_ref[...]).astype(BF16)
    x = x_ref[...] + _dot(y_sb, w_ref[:d_sb, :]) + _dot(y_dil, w_ref[d_sb:, :])
    xo_ref[...] = x
    xn = norm(x, gffn_ref[...])
    xn_ref[...] = xn.astype(BF16)

    if with_router:
        lane = lax.broadcasted_iota(jnp.int32, comb_ref.shape, 1)
        logits = jnp.full(comb_ref.shape, -jnp.inf, F32)
        for e in range(N_EXPERTS):
            le = jnp.sum(xn * wr_ref[e:e + 1, :], axis=-1, keepdims=True)
            logits = jnp.where(lane == e, le, logits)
        top1 = jnp.max(logits, axis=-1, keepdims=True)
        idx1 = jnp.min(jnp.where(logits == top1, lane, V7X_LANES), axis=-1, keepdims=True)
        rest = jnp.where(lane == idx1, -jnp.inf, logits)
        top2 = jnp.max(rest, axis=-1, keepdims=True)
        idx2 = jnp.min(jnp.where(rest == top2, lane, V7X_LANES), axis=-1, keepdims=True)
        t = jnp.exp(top2 - top1)
        gate1 = 1.0 / (1.0 + t)
        gate2 = t / (1.0 + t)
        comb = jnp.where(lane == idx1, gate1, 0.0) + jnp.where(lane == idx2, gate2, 0.0)
        comb_ref[...] = comb
        routed = jnp.sum(jnp.where(comb > 0.0, 1.0, 0.0), axis=0, keepdims=True)
        cnt_ref[...] = jnp.broadcast_to(routed, cnt_ref.shape)


def out_projection(o_sb, o_dil, x, g_sb, g_dil, w_out, g_ffn, w_router_t=None, tm=512):
    n, d = x.shape
    d_sb = o_sb.shape[1]
    d_mix = w_out.shape[0]
    with_router = w_router_t is not None
    row = lambda width: pl.BlockSpec((tm, width), lambda i: (i, 0))
    const = lambda shape: pl.BlockSpec(shape, lambda i: (0, 0))
    in_specs = [row(d_sb), row(d_mix - d_sb), row(d), const((1, d_sb)), const((1, d_mix - d_sb)),
                const((d_mix, d)), const((1, d))]
    args = [o_sb, o_dil, x, g_sb, g_dil, w_out, g_ffn]
    out_specs = [row(d), row(d)]
    out_shape = [jax.ShapeDtypeStruct((n, d), F32), jax.ShapeDtypeStruct((n, d), BF16)]
    if with_router:
        in_specs.append(const(w_router_t.shape))
        args.append(w_router_t)
        out_specs += [row(V7X_LANES), pl.BlockSpec((V7X_SUBLANES, V7X_LANES), lambda i: (i, 0))]
        out_shape += [jax.ShapeDtypeStruct((n, V7X_LANES), F32),
                      jax.ShapeDtypeStruct((n // tm * V7X_SUBLANES, V7X_LANES), F32)]
    return pl.pallas_call(
        functools.partial(_outproj_kernel, with_router=with_router),
        grid=(n // tm,),
        in_specs=in_specs,
        out_specs=out_specs,
        out_shape=out_shape,
        compiler_params=_params("parallel"),
    )(*args)


def _ffn_kernel(xn_ref, x_ref, wg_ref, wu_ref, wd_ref, o_ref):
    @pl.when(pl.program_id(1) == 0)
    def _():
        o_ref[...] = x_ref[...]

    xn = xn_ref[...]
    gate = _dot(xn, wg_ref[...])
    up = _dot(xn, wu_ref[...])
    hid = gate * (1.0 / (1.0 + jnp.exp(-gate))) * up
    o_ref[...] += _dot(hid.astype(BF16), wd_ref[...])


def swiglu_ffn(xn, x, wg, wu, wd, tm=512, tf=512):
    n, d = x.shape
    width = wg.shape[1]
    assert width % tf == 0
    row = lambda width_: pl.BlockSpec((tm, width_), lambda i, f: (i, 0))
    return pl.pallas_call(
        _ffn_kernel,
        grid=(n // tm, width // tf),
        in_specs=[row(d), row(d),
                  pl.BlockSpec((d, tf), lambda i, f: (0, f)),
                  pl.BlockSpec((d, tf), lambda i, f: (0, f)),
                  pl.BlockSpec((tf, d), lambda i, f: (f, 0))],
        out_specs=row(d),
        out_shape=jax.ShapeDtypeStruct((n, d), F32),
        compiler_params=_params("parallel", "arbitrary"),
    )(xn, x, wg, wu, wd)


MOE_ROW_TILE = 512
MOE_SEG_BITS = (32, 16, 8, 4, 2, 1)


def _moe_layout(counts, tm, n_rows):
    n_exp = counts.shape[1]
    n16 = (counts + V7X_BF16_ROWS - 1) // V7X_BF16_ROWS
    rows = n16 * V7X_BF16_ROWS
    region = jnp.sum(rows, axis=0)
    region = (region + MOE_ROW_TILE - 1) // MOE_ROW_TILE * MOE_ROW_TILE
    ends = jnp.cumsum(region)
    tile_off = (ends - region)[None, :] + jnp.cumsum(rows, axis=0) - rows
    row_start = jnp.arange(n_rows // MOE_ROW_TILE, dtype=jnp.int32) * MOE_ROW_TILE
    tile_expert = jnp.minimum(jnp.sum(row_start[:, None] >= ends[None, :], axis=1), n_exp - 1)
    n_used = ends[-1:] // MOE_ROW_TILE
    i32 = lambda a: a.astype(jnp.int32)
    return i32(tile_off).reshape(-1), i32(n16).reshape(-1), i32(tile_expert), i32(n_used)


def _moe_tile_segments(n16_ref, tile):
    segs, start = [], jnp.int32(0)
    for e in range(N_EXPERTS):
        pieces = n16_ref[tile * N_EXPERTS + e]
        segs.append((start, pieces))
        start = start + pieces * V7X_BF16_ROWS
    return segs


def _moe_tile_slots(comb, segs, n_slots):
    tm = comb.shape[0]
    rows_e = V7X_BF16_ROWS
    comb_t = comb.T[:rows_e, :]
    routed = comb_t > 0.0
    before = (lax.broadcasted_iota(jnp.int32, (tm, tm), 0)
              < lax.broadcasted_iota(jnp.int32, (tm, tm), 1))
    rank = _dot(jnp.where(routed, 1.0, 0.0).astype(BF16),
                jnp.where(before, 1.0, 0.0).astype(BF16))
    sub = lax.broadcasted_iota(jnp.int32, (rows_e, tm), 0)
    seg_col = jnp.zeros((rows_e, tm), F32)
    for e, (start, _) in enumerate(segs):
        seg_col = jnp.where(sub == e, start.astype(F32), seg_col)
    slot = jnp.where(routed, rank + seg_col, -1.0)
    n_routed = jnp.sum(jnp.where(routed, 1.0, 0.0), axis=0, keepdims=True)
    slot_hi = jnp.max(slot, axis=0, keepdims=True)
    slot_lo = jnp.where(n_routed >= 2.0,
                        jnp.sum(jnp.where(routed, slot, 0.0), axis=0, keepdims=True) - slot_hi,
                        -1.0)
    gate_hi = jnp.sum(jnp.where(slot == slot_hi, comb_t, 0.0), axis=0, keepdims=True)
    gate_lo = jnp.sum(comb_t, axis=0, keepdims=True) - gate_hi
    s_idx = lax.broadcasted_iota(jnp.int32, (n_slots, tm), 0).astype(F32)
    hit_hi = s_idx == slot_hi
    hit_lo = s_idx == slot_lo
    onehot = jnp.where(hit_hi, 1.0, jnp.where(hit_lo, 1.0, 0.0)).astype(BF16)
    gate_slot = jnp.sum(jnp.where(hit_hi, gate_hi, jnp.where(hit_lo, gate_lo, 0.0)),
                        axis=1, keepdims=True)
    return onehot, gate_slot


def _segment_copies(vmem_ref, hbm_ref, segs, off_ref, tile, sems, to_hbm):
    copies = []
    for e, (seg_start, pieces) in enumerate(segs):
        hbm_start = off_ref[tile * N_EXPERTS + e]
        for bit in MOE_SEG_BITS:
            done = (pieces // (2 * bit)) * (2 * bit) * V7X_BF16_ROWS
            rows = bit * V7X_BF16_ROWS
            local = vmem_ref.at[pl.ds(pl.multiple_of(seg_start + done, V7X_BF16_ROWS), rows)]
            remote = hbm_ref.at[pl.ds(pl.multiple_of(hbm_start + done, V7X_BF16_ROWS), rows)]
            src, dst = (local, remote) if to_hbm else (remote, local)
            copies.append(((pieces & bit) != 0, pltpu.make_async_copy(src, dst, sems.at[e])))
    return copies


def _start_all(copies):
    for pred, cp in copies:
        pl.when(pred)(cp.start)


def _wait_all(copies):
    for pred, cp in copies:
        pl.when(pred)(cp.wait)


def _moe_dispatch_kernel(off_ref, n16_ref, xn_ref, comb_ref, xs_in_ref, xs_ref, buf_ref, sems):
    del xs_in_ref
    tile = pl.program_id(0)
    d = xn_ref.shape[1]
    n_slots = buf_ref.shape[0]
    segs = _moe_tile_segments(n16_ref, tile)
    onehot, gate_slot = _moe_tile_slots(comb_ref[...], segs, n_slots)
    buf_ref[:, :d] = _dot(onehot, xn_ref[...]).astype(BF16)
    g_hi = gate_slot.astype(BF16).astype(F32)
    g_lo = gate_slot - g_hi
    lane = lax.broadcasted_iota(jnp.int32, (n_slots, V7X_LANES), 1)
    buf_ref[:, d:] = jnp.where(lane == 0, g_hi, jnp.where(lane == 1, g_lo, 0.0)).astype(BF16)
    copies = _segment_copies(buf_ref, xs_ref, segs, off_ref, tile, sems, to_hbm=True)
    _start_all(copies)
    _wait_all(copies)


def _moe_expert_kernel(te_ref, used_ref, xs_ref, wg_ref, wu_ref, wd_ref, ys_ref):
    d = wg_ref.shape[1]
    live = pl.program_id(0) < used_ref[0]

    @pl.when(live)
    def _():
        xs = xs_ref[:, :d]
        weight = xs_ref[:, d:d + 1].astype(F32) + xs_ref[:, d + 1:d + 2].astype(F32)
        gate = _dot(xs, wg_ref[0])
        up = _dot(xs, wu_ref[0])
        hid = gate * (1.0 / (1.0 + jnp.exp(-gate))) * up * weight
        ys_ref[...] = _dot(hid.astype(BF16), wd_ref[0]).astype(BF16)

    @pl.when(jnp.logical_not(live))
    def _():
        ys_ref[...] = jnp.zeros_like(ys_ref)


def _moe_combine_kernel(off_ref, n16_ref, ys_ref, comb_ref, x_ref, o_ref, buf_ref, sems):
    tile = pl.program_id(0)
    n_slots = buf_ref.shape[0]

    @pl.when(tile == 0)
    def _():
        buf_ref[...] = jnp.zeros_like(buf_ref)

    segs = _moe_tile_segments(n16_ref, tile)
    copies = _segment_copies(buf_ref, ys_ref, segs, off_ref, tile, sems, to_hbm=False)
    _start_all(copies)
    onehot, _ = _moe_tile_slots(comb_ref[...], segs, n_slots)
    _wait_all(copies)
    o_ref[...] = x_ref[...] + lax.dot_general(onehot, buf_ref[...], (((0,), (0,)), ((), ())),
                                              preferred_element_type=F32)


def moe_ffn(xn, x, comb, counts, wg, wu, wd, tm=512):
    n, d = x.shape
    n_exp, _, width = wg.shape
    n_tiles = n // tm
    seg_pad = n_exp * (V7X_BF16_ROWS - 1)
    n_slots = -(-(2 * tm + seg_pad) // V7X_LANES) * V7X_LANES
    n_rows = 2 * n + n_tiles * seg_pad + n_exp * (MOE_ROW_TILE - 1)
    n_rows = -(-n_rows // MOE_ROW_TILE) * MOE_ROW_TILE
    tile_off, n16, tile_expert, n_used = _moe_layout(counts, tm, n_rows)
    d_x = d + V7X_LANES
    any_spec = pl.BlockSpec(memory_space=pl.ANY)
    sems = pltpu.SemaphoreType.DMA((n_exp,))

    xs = pl.pallas_call(
        _moe_dispatch_kernel,
        grid_spec=pltpu.PrefetchScalarGridSpec(
            num_scalar_prefetch=2, grid=(n_tiles,),
            in_specs=[pl.BlockSpec((tm, d), lambda i, *_: (i, 0)),
                      pl.BlockSpec((tm, V7X_LANES), lambda i, *_: (i, 0)), any_spec],
            out_specs=any_spec,
            scratch_shapes=[pltpu.VMEM((n_slots, d_x), BF16), sems]),
        out_shape=jax.ShapeDtypeStruct((n_rows, d_x), BF16),
        input_output_aliases={4: 0},
        compiler_params=_params("arbitrary"),
    )(tile_off, n16, xn, comb, jnp.zeros((n_rows, d_x), BF16))

    once = pl.Buffered(1)
    ys = pl.pallas_call(
        _moe_expert_kernel,
        grid_spec=pltpu.PrefetchScalarGridSpec(
            num_scalar_prefetch=2, grid=(n_rows // MOE_ROW_TILE,),
            in_specs=[pl.BlockSpec((MOE_ROW_TILE, d_x), lambda r, te, used: (r, 0)),
                      pl.BlockSpec((1, d, width), lambda r, te, used: (te[r], 0, 0),
                                   pipeline_mode=once),
                      pl.BlockSpec((1, d, width), lambda r, te, used: (te[r], 0, 0),
                                   pipeline_mode=once),
                      pl.BlockSpec((1, width, d), lambda r, te, used: (te[r], 0, 0),
                                   pipeline_mode=once)],
            out_specs=pl.BlockSpec((MOE_ROW_TILE, d), lambda r, te, used: (r, 0))),
        out_shape=jax.ShapeDtypeStruct((n_rows, d), BF16),
        compiler_params=_params("arbitrary"),
    )(tile_expert, n_used, xs, wg, wu, wd)

    return pl.pallas_call(
        _moe_combine_kernel,
        grid_spec=pltpu.PrefetchScalarGridSpec(
            num_scalar_prefetch=2, grid=(n_tiles,),
            in_specs=[any_spec, pl.BlockSpec((tm, V7X_LANES), lambda i, *_: (i, 0)),
                      pl.BlockSpec((tm, d), lambda i, *_: (i, 0))],
            out_specs=pl.BlockSpec((tm, d), lambda i, *_: (i, 0)),
            scratch_shapes=[pltpu.VMEM((n_slots, d), BF16), sems]),
        out_shape=jax.ShapeDtypeStruct((n, d), F32),
        compiler_params=_params("arbitrary"),
    )(tile_off, n16, ys, comb, x)


def _slab_gains(q_sb, k_sb, q_dil, k_dil):
    ones = jnp.ones((4, HEAD_DIM), F32)
    rep = lambda g: jnp.broadcast_to(g[None, :], (4, HEAD_DIM))
    return jnp.concatenate([rep(q_sb), rep(k_sb), ones, rep(q_dil), rep(k_dil), ones], axis=0)


def kernel(x, positions, norm_mix, w_in, q_norm_sb, k_norm_sb, q_norm_dil, k_norm_dil, out_norm_sb, out_norm_dil, w_out, norm_ffn, w_gate_dense, w_up_dense, w_down_dense, w_router, w_gate_moe, w_up_moe, w_down_moe):
    batch, seq, d = x.shape
    n = batch * seq
    depth = w_in.shape[0]
    xf = x.reshape(n, d)
    pos_b = jnp.broadcast_to(positions.reshape(n, 1), (n, HEAD_DIM))
    cos, sin = rope_tables(pos_b)
    for i in range(depth):
        gains = _slab_gains(q_norm_sb[i], k_norm_sb[i], q_norm_dil[i], k_norm_dil[i])
        qkv = in_projection(xf, norm_mix[i][None, :], w_in[i].astype(BF16), gains, cos, sin, batch)
        o_sb = stick_breaking_attention(qkv)
        o_dil = dilated_attention(qkv)
        j = i // 2
        moe_layer = i % 2 == 1
        outs = out_projection(
            o_sb, o_dil, xf, out_norm_sb[i][None, :], out_norm_dil[i][None, :],
            w_out[i].astype(BF16), norm_ffn[i][None, :],
            w_router[j].T if moe_layer else None)
        if moe_layer:
            xf, xn, comb, cnt = outs
            counts = cnt[::V7X_SUBLANES, :N_EXPERTS].astype(jnp.int32)
            xf = moe_ffn(xn, xf, comb, counts, w_gate_moe[j].astype(BF16),
                         w_up_moe[j].astype(BF16), w_down_moe[j].astype(BF16))
        else:
            xf, xn = outs
            xf = swiglu_ffn(xn, xf, w_gate_dense[j].astype(BF16), w_up_dense[j].astype(BF16),
                            w_down_dense[j].astype(BF16))
    return xf.reshape(batch, seq, d)
```

```python
import functools
import math

import jax
import jax.numpy as jnp
from jax import lax
from jax.experimental import pallas as pl
from jax.experimental.pallas import tpu as pltpu

HEAD_DIM = 128
N_HEADS_SB = 4
N_HEADS_DIL = 4
N_SLABS = 3 * (N_HEADS_SB + N_HEADS_DIL)
SLAB_Q_SB, SLAB_K_SB, SLAB_V_SB = 0, 4, 8
SLAB_Q_DIL, SLAB_K_DIL, SLAB_V_DIL = 12, 16, 20
DILATED_CONFIGS = ((128, 1), (512, 4), (2048, 16))
DIL_STEPS = 128
N_EXPERTS = 8
ROPE_THETA = 10000.0
EPS = 1e-6
NEG_INF = -1e30
SB_ZERO_WEIGHT_LOG = 112.0

V7X_LANES = 128
V7X_SUBLANES = 8
V7X_BF16_ROWS = 16
V7X_VMEM_LIMIT_BYTES = 56 * 1024 * 1024

F32 = jnp.float32
BF16 = jnp.bfloat16


def _params(*semantics):
    return pltpu.CompilerParams(dimension_semantics=semantics,
                                vmem_limit_bytes=V7X_VMEM_LIMIT_BYTES)


def _dot(a, b):
    return jnp.dot(a, b, preferred_element_type=F32)


def _dot_nt(a, b):
    return lax.dot_general(a, b, (((1,), (1,)), ((), ())), preferred_element_type=F32)


def _rope_kernel(pos_ref, cos_ref, sin_ref):
    lane = lax.broadcasted_iota(jnp.int32, pos_ref.shape, 1)
    half = HEAD_DIM // 2
    pair = (lane % half).astype(F32)
    inv = jnp.exp(pair * (-2.0 * math.log(ROPE_THETA) / HEAD_DIM))
    ang = pos_ref[...].astype(F32) * inv
    cos_ref[...] = jnp.cos(ang)
    s = jnp.sin(ang)
    sin_ref[...] = jnp.where(lane < half, -s, s)


def rope_tables(pos_b, tm=1024):
    n = pos_b.shape[0]
    spec = pl.BlockSpec((tm, HEAD_DIM), lambda i: (i, 0))
    return pl.pallas_call(
        _rope_kernel,
        grid=(n // tm,),
        in_specs=[spec],
        out_specs=[spec, spec],
        out_shape=[jax.ShapeDtypeStruct((n, HEAD_DIM), F32)] * 2,
        compiler_params=_params("parallel"),
    )(pos_b)


def _inproj_kernel(x_ref, g_ref, w_ref, gains_ref, cos_ref, sin_ref, o_ref):
    x = x_ref[...]
    ms = jnp.mean(x * x, axis=-1, keepdims=True)
    xn = (x * lax.rsqrt(ms + EPS) * g_ref[...]).astype(BF16)
    cos = cos_ref[...]
    sin = sin_ref[...]
    scale = 1.0 / math.sqrt(HEAD_DIM)
    group = 4
    for j0 in range(0, N_SLABS, group):
        h = _dot(xn, w_ref[:, j0 * HEAD_DIM:(j0 + group) * HEAD_DIM])
        for j in range(j0, j0 + group):
            y = h[:, (j - j0) * HEAD_DIM:(j - j0 + 1) * HEAD_DIM]
            is_v = SLAB_V_SB <= j < SLAB_Q_DIL or j >= SLAB_V_DIL
            if not is_v:
                ms_h = jnp.mean(y * y, axis=-1, keepdims=True)
                y = y * lax.rsqrt(ms_h + EPS) * gains_ref[j:j + 1, :]
                if j >= SLAB_Q_DIL:
                    y = y * cos + pltpu.roll(y, HEAD_DIM // 2, 1) * sin
                if j < SLAB_K_SB or SLAB_Q_DIL <= j < SLAB_K_DIL:
                    y = y * scale
            o_ref[0, j] = y.astype(BF16)


def in_projection(x, g_norm, w_in, gains, cos, sin, batch, tm=512):
    n, d = x.shape
    s = n // batch
    d_in = w_in.shape[1]
    nt = s // tm
    return pl.pallas_call(
        _inproj_kernel,
        grid=(batch, nt),
        in_specs=[
            pl.BlockSpec((tm, d), lambda b, i: (b * nt + i, 0)),
            pl.BlockSpec((1, d), lambda b, i: (0, 0)),
            pl.BlockSpec((d, d_in), lambda b, i: (0, 0)),
            pl.BlockSpec((N_SLABS, HEAD_DIM), lambda b, i: (0, 0)),
            pl.BlockSpec((tm, HEAD_DIM), lambda b, i: (b * nt + i, 0)),
            pl.BlockSpec((tm, HEAD_DIM), lambda b, i: (b * nt + i, 0)),
        ],
        out_specs=pl.BlockSpec((1, N_SLABS, tm, HEAD_DIM), lambda b, i: (b, 0, i, 0)),
        out_shape=jax.ShapeDtypeStruct((batch, N_SLABS, s, HEAD_DIM), BF16),
        compiler_params=_params("parallel", "parallel"),
    )(x, g_norm, w_in, gains, cos, sin)


def _softplus(z):
    return jnp.maximum(z, 0.0) + jnp.log(1.0 + jnp.exp(-jnp.abs(z)))


def _sb_kernel(q_ref, k_ref, v_ref, o_ref, *, blk, nsub):
    i = pl.program_id(2)
    row = lax.broadcasted_iota(jnp.int32, (blk, blk), 0)
    col = lax.broadcasted_iota(jnp.int32, (blk, blk), 1)
    incl = (row >= col).astype(BF16)
    strict = col < row

    def block(q, c, later, acc, diagonal):
        start = pl.multiple_of(jnp.maximum(c, 0) * blk, blk)
        kb = k_ref[0, 0, pl.ds(start, blk), :]
        vb = v_ref[0, 0, pl.ds(start, blk), :]
        z = _dot_nt(q, kb)
        sp = _softplus(z)
        if diagonal:
            sp = jnp.where(strict, sp, 0.0)
        sp_hi = sp.astype(BF16)
        sp_lo = (sp - sp_hi.astype(F32)).astype(BF16)
        r = _dot(sp_hi, incl) + _dot(sp_lo, incl)
        a = jnp.exp(z - r - later)
        total = r[:, :1]
        if diagonal:
            a = jnp.where(strict, a, 0.0)
        else:
            a = jnp.where(c >= 0, a, 0.0)
            total = jnp.where(c >= 0, total, 0.0)
        return later + total, acc + _dot(a.astype(BF16), vb)

    qs = [q_ref[0, 0, j * blk:(j + 1) * blk, :] for j in range(nsub)]
    own = [i * nsub + j for j in range(nsub)]
    state = [block(qs[j], own[j], jnp.zeros((blk, 1), F32), jnp.zeros((blk, HEAD_DIM), F32), True)
             for j in range(nsub)]

    def unfinished(back, laters):
        flags = [jnp.logical_and(own[j] - back >= 0, jnp.min(laters[j]) < SB_ZERO_WEIGHT_LOG)
                 for j in range(nsub)]
        return functools.reduce(jnp.logical_or, flags)

    def cond(carry):
        return carry[1]

    def body(carry):
        back, _, laters, accs = carry
        new = [block(qs[j], own[j] - back, laters[j], accs[j], False) for j in range(nsub)]
        laters, accs = [s[0] for s in new], [s[1] for s in new]
        return back + 1, unfinished(back + 1, laters), laters, accs

    state = [block(qs[j], own[j] - 1, state[j][0], state[j][1], False) for j in range(nsub)]
    laters, accs = [s[0] for s in state], [s[1] for s in state]
    _, _, _, accs = lax.while_loop(cond, body, (2, unfinished(2, laters), laters, accs))
    for j in range(nsub):
        o_ref[0, j * blk:(j + 1) * blk, :] = accs[j].astype(o_ref.dtype)


def stick_breaking_attention(qkv, blk=256, nsub=4):
    batch, _, s, dh = qkv.shape
    nh = N_HEADS_SB
    o = pl.pallas_call(
        functools.partial(_sb_kernel, blk=blk, nsub=nsub),
        grid=(batch, nh, s // (blk * nsub)),
        in_specs=[
            pl.BlockSpec((1, 1, blk * nsub, dh), lambda b, h, i: (b, SLAB_Q_SB + h, i, 0)),
            pl.BlockSpec((1, 1, s, dh), lambda b, h, i: (b, SLAB_K_SB + h, 0, 0)),
            pl.BlockSpec((1, 1, s, dh), lambda b, h, i: (b, SLAB_V_SB + h, 0, 0)),
        ],
        out_specs=pl.BlockSpec((1, blk * nsub, dh), lambda b, h, i: (b, i, h)),
        out_shape=jax.ShapeDtypeStruct((batch, s, nh * dh), BF16),
        compiler_params=_params("parallel", "parallel", "arbitrary"),
    )(qkv, qkv, qkv)
    return o.reshape(batch * s, nh * dh)


def _dil_kernel(q_ref, k_ref, v_ref, o_ref, kf_ref, vf_ref, qf_ref, po_ref, pm_ref, pl_ref):
    i = pl.program_id(2)
    st = DIL_STEPS
    span, dh = qf_ref.shape
    seq = kf_ref.shape[0]
    chunk = 512

    @pl.when(i == 0)
    def _():
        def conv(c, carry):
            rows = pl.ds(pl.multiple_of(c * chunk, chunk), chunk)
            kf_ref[rows, :] = k_ref[0, 0, rows, :].astype(F32)
            vf_ref[rows, :] = v_ref[0, 0, rows, :].astype(F32)
            return carry
        lax.fori_loop(0, seq // chunk, conv, 0)

    qf_ref[...] = q_ref[0, 0].astype(F32)

    diff = (lax.broadcasted_iota(jnp.int32, (st, 2 * st), 1)
            - lax.broadcasted_iota(jnp.int32, (st, 2 * st), 0))
    bias_normal = jnp.where((diff >= 0) & (diff <= st), 0.0, NEG_INF)
    bias_first = jnp.where(diff <= 0, 0.0, NEG_INF)
    ones = jnp.ones((2 * st, dh), BF16)

    def strided(start, size, dil):
        return pl.ds(start, size) if dil == 1 else pl.ds(start, size, stride=dil)

    for c, (_, dil) in enumerate(DILATED_CONFIGS):
        blk_rows = st * dil

        def unit(u, carry, c=c, dil=dil, blk_rows=blk_rows):
            r = u % dil
            q_start = (u // dil) * blk_rows + r
            p0 = i * span + (u // dil) * blk_rows
            first = p0 == 0
            w_start = jnp.where(first, r, p0 - blk_rows + r)
            q = qf_ref[strided(q_start, st, dil), :].astype(BF16)
            kw = kf_ref[strided(w_start, 2 * st, dil), :].astype(BF16)
            vw = vf_ref[strided(w_start, 2 * st, dil), :].astype(BF16)
            s = _dot_nt(q, kw) + jnp.where(first, bias_first, bias_normal)
            m = jnp.max(s, axis=-1, keepdims=True)
            p = jnp.exp(s - m).astype(BF16)
            o_den = _dot(p, jnp.concatenate([vw, ones], axis=1))
            rows = strided(q_start, st, dil)
            po_ref[c, rows, :] = o_den[:, :dh]
            pl_ref[c, rows, :] = o_den[:, dh:]
            pm_ref[c, rows, :] = jnp.broadcast_to(m, (st, dh))
            return carry

        lax.fori_loop(0, span // st, unit, 0, unroll=8)

    def merge(c, carry):
        rows = pl.ds(pl.multiple_of(c * st, st), st)
        ms = [pm_ref[g, rows, :] for g in range(len(DILATED_CONFIGS))]
        m = functools.reduce(jnp.maximum, ms)
        ws = [jnp.exp(mg - m) for mg in ms]
        den = sum(w * pl_ref[g, rows, :] for g, w in enumerate(ws))
        num = sum(w * po_ref[g, rows, :] for g, w in enumerate(ws))
        o_ref[0, rows, :] = (num / den).astype(o_ref.dtype)
        return carry

    lax.fori_loop(0, span // st, merge, 0, unroll=2)


def dilated_attention(qkv):
    batch, _, s, dh = qkv.shape
    nh = N_HEADS_DIL
    span = DIL_STEPS * max(dil for _, dil in DILATED_CONFIGS)
    assert s % span == 0 and s >= 2 * span
    n_cfg = len(DILATED_CONFIGS)
    o = pl.pallas_call(
        _dil_kernel,
        grid=(batch, nh, s // span),
        in_specs=[
            pl.BlockSpec((1, 1, span, dh), lambda b, h, i: (b, SLAB_Q_DIL + h, i, 0)),
            pl.BlockSpec((1, 1, s, dh), lambda b, h, i: (b, SLAB_K_DIL + h, 0, 0)),
            pl.BlockSpec((1, 1, s, dh), lambda b, h, i: (b, SLAB_V_DIL + h, 0, 0)),
        ],
        out_specs=pl.BlockSpec((1, span, dh), lambda b, h, i: (b, i, h)),
        out_shape=jax.ShapeDtypeStruct((batch, s, nh * dh), BF16),
        scratch_shapes=[
            pltpu.VMEM((s, dh), F32), pltpu.VMEM((s, dh), F32), pltpu.VMEM((span, dh), F32),
            pltpu.VMEM((n_cfg, span, dh), F32), pltpu.VMEM((n_cfg, span, dh), F32),
            pltpu.VMEM((n_cfg, span, dh), F32)],
        compiler_params=_params("parallel", "parallel", "arbitrary"),
    )(qkv, qkv, qkv)
    return o.reshape(batch * s, nh * dh)


def _outproj_kernel(*refs, with_router):
    osb_ref, odil_ref, x_ref, gsb_ref, gdil_ref, w_ref, gffn_ref = refs[:7]
    if with_router:
        wr_ref, xo_ref, xn_ref, comb_ref, cnt_ref = refs[7:]
    else:
        xo_ref, xn_ref = refs[7:]

    def norm(y, g):
        return y * lax.rsqrt(jnp.mean(y * y, axis=-1, keepdims=True) + EPS) * g

    d_sb = osb_ref.shape[1]
    y_sb = norm(osb_ref[...].astype(F32), gsb_ref[...]).astype(BF16)
    y_dil = norm(odil_ref[...].astype(F32), gdil_ref[...]).astype(BF16)
    x = x_ref[...] + _dot(y_sb, w_ref[:d_sb, :]) + _dot(y_dil, w_ref[d_sb:, :])
    xo_ref[...] = x
    xn = norm(x, gffn_ref[...])
    xn_ref[...] = xn.astype(BF16)

    if with_router:
        lane = lax.broadcasted_iota(jnp.int32, comb_ref.shape, 1)
        logits = jnp.full(comb_ref.shape, -jnp.inf, F32)
        for e in range(N_EXPERTS):
            le = jnp.sum(xn * wr_ref[e:e + 1, :], axis=-1, keepdims=True)
            logits = jnp.where(lane == e, le, logits)
        top1 = jnp.max(logits, axis=-1, keepdims=True)
        idx1 = jnp.min(jnp.where(logits == top1, lane, V7X_LANES), axis=-1, keepdims=True)
        rest = jnp.where(lane == idx1, -jnp.inf, logits)
        top2 = jnp.max(rest, axis=-1, keepdims=True)
        idx2 = jnp.min(jnp.where(rest == top2, lane, V7X_LANES), axis=-1, keepdims=True)
        t = jnp.exp(top2 - top1)
        gate1 = 1.0 / (1.0 + t)
        gate2 = t / (1.0 + t)
        comb = jnp.where(lane == idx1, gate1, 0.0) + jnp.where(lane == idx2, gate2, 0.0)
        comb_ref[...] = comb
        routed = jnp.sum(jnp.where(comb > 0.0, 1.0, 0.0), axis=0, keepdims=True)
        cnt_ref[...] = jnp.broadcast_to(routed, cnt_ref.shape)


def out_projection(o_sb, o_dil, x, g_sb, g_dil, w_out, g_ffn, w_router_t=None, tm=512):
    n, d = x.shape
    d_sb = o_sb.shape[1]
    d_mix = w_out.shape[0]
    with_router = w_router_t is not None
    row = lambda width: pl.BlockSpec((tm, width), lambda i: (i, 0))
    const = lambda shape: pl.BlockSpec(shape, lambda i: (0, 0))
    in_specs = [row(d_sb), row(d_mix - d_sb), row(d), const((1, d_sb)), const((1, d_mix - d_sb)),
                const((d_mix, d)), const((1, d))]
    args = [o_sb, o_dil, x, g_sb, g_dil, w_out, g_ffn]
    out_specs = [row(d), row(d)]
    out_shape = [jax.ShapeDtypeStruct((n, d), F32), jax.ShapeDtypeStruct((n, d), BF16)]
    if with_router:
        in_specs.append(const(w_router_t.shape))
        args.append(w_router_t)
        out_specs += [row(V7X_LANES), pl.BlockSpec((V7X_SUBLANES, V7X_LANES), lambda i: (i, 0))]
        out_shape += [jax.ShapeDtypeStruct((n, V7X_LANES), F32),
                      jax.ShapeDtypeStruct((n // tm * V7X_SUBLANES, V7X_LANES), F32)]
    return pl.pallas_call(
        functools.partial(_outproj_kernel, with_router=with_router),
        grid=(n // tm,),
        in_specs=in_specs,
        out_specs=out_specs,
        out_shape=out_shape,
        compiler_params=_params("parallel"),
    )(*args)


def _ffn_kernel(xn_ref, x_ref, wg_ref, wu_ref, wd_ref, o_ref):
    @pl.when(pl.program_id(1) == 0)
    def _():
        o_ref[...] = x_ref[...]

    xn = xn_ref[...]
    gate = _dot(xn, wg_ref[...])
    up = _dot(xn, wu_ref[...])
    hid = gate * (1.0 / (1.0 + jnp.exp(-gate))) * up
    o_ref[...] += _dot(hid.astype(BF16), wd_ref[...])


def swiglu_ffn(xn, x, wg, wu, wd, tm=512, tf=512):
    n, d = x.shape
    width = wg.shape[1]
    assert width % tf == 0
    row = lambda width_: pl.BlockSpec((tm, width_), lambda i, f: (i, 0))
    return pl.pallas_call(
        _ffn_kernel,
        grid=(n // tm, width // tf),
        in_specs=[row(d), row(d),
                  pl.BlockSpec((d, tf), lambda i, f: (0, f)),
                  pl.BlockSpec((d, tf), lambda i, f: (0, f)),
                  pl.BlockSpec((tf, d), lambda i, f: (f, 0))],
        out_specs=row(d),
        out_shape=jax.ShapeDtypeStruct((n, d), F32),
        compiler_params=_params("parallel", "arbitrary"),
    )(xn, x, wg, wu, wd)


MOE_ROW_TILE = 512
MOE_SEG_BITS = (32, 16, 8, 4, 2, 1)


def _moe_layout(counts, tm, n_rows):
    n_exp = counts.shape[1]
    n16 = (counts + V7X_BF16_ROWS - 1) // V7X_BF16_ROWS
    rows = n16 * V7X_BF16_ROWS
    region = jnp.sum(rows, axis=0)
    region = (region + MOE_ROW_TILE - 1) // MOE_ROW_TILE * MOE_ROW_TILE
    ends = jnp.cumsum(region)
    tile_off = (ends - region)[None, :] + jnp.cumsum(rows, axis=0) - rows
    row_start = jnp.arange(n_rows // MOE_ROW_TILE, dtype=jnp.int32) * MOE_ROW_TILE
    tile_expert = jnp.minimum(jnp.sum(row_start[:, None] >= ends[None, :], axis=1), n_exp - 1)
    n_used = ends[-1:] // MOE_ROW_TILE
    i32 = lambda a: a.astype(jnp.int32)
    return i32(tile_off).reshape(-1), i32(n16).reshape(-1), i32(tile_expert), i32(n_used)


def _moe_tile_segments(n16_ref, tile):
    segs, start = [], jnp.int32(0)
    for e in range(N_EXPERTS):
        pieces = n16_ref[tile * N_EXPERTS + e]
        segs.append((start, pieces))
        start = start + pieces * V7X_BF16_ROWS
    return segs


def _moe_tile_slots(comb, segs, n_slots):
    tm = comb.shape[0]
    rows_e = V7X_BF16_ROWS
    comb_t = comb.T[:rows_e, :]
    routed = comb_t > 0.0
    before = (lax.broadcasted_iota(jnp.int32, (tm, tm), 0)
              < lax.broadcasted_iota(jnp.int32, (tm, tm), 1))
    rank = _dot(jnp.where(routed, 1.0, 0.0).astype(BF16),
                jnp.where(before, 1.0, 0.0).astype(BF16))
    sub = lax.broadcasted_iota(jnp.int32, (rows_e, tm), 0)
    seg_col = jnp.zeros((rows_e, tm), F32)
    for e, (start, _) in enumerate(segs):
        seg_col = jnp.where(sub == e, start.astype(F32), seg_col)
    slot = jnp.where(routed, rank + seg_col, -1.0)
    n_routed = jnp.sum(jnp.where(routed, 1.0, 0.0), axis=0, keepdims=True)
    slot_hi = jnp.max(slot, axis=0, keepdims=True)
    slot_lo = jnp.where(n_routed >= 2.0,
                        jnp.sum(jnp.where(routed, slot, 0.0), axis=0, keepdims=True) - slot_hi,
                        -1.0)
    gate_hi = jnp.sum(jnp.where(slot == slot_hi, comb_t, 0.0), axis=0, keepdims=True)
    gate_lo = jnp.sum(comb_t, axis=0, keepdims=True) - gate_hi
    s_idx = lax.broadcasted_iota(jnp.int32, (n_slots, tm), 0).astype(F32)
    hit_hi = s_idx == slot_hi
    hit_lo = s_idx == slot_lo
    onehot = jnp.where(hit_hi, 1.0, jnp.where(hit_lo, 1.0, 0.0)).astype(BF16)
    gate_slot = jnp.sum(jnp.where(hit_hi, gate_hi, jnp.where(hit_lo, gate_lo, 0.0)),
                        axis=1, keepdims=True)
    return onehot, gate_slot


def _segment_copies(buf_ref, hbm_ref, off_ref, n16_ref, tile, sems, to_hbm):
    slot = tile % 2
    copies = []
    for e, (seg_start, pieces) in enumerate(_moe_tile_segments(n16_ref, tile)):
        hbm_start = off_ref[tile * N_EXPERTS + e]
        for bit in MOE_SEG_BITS:
            done = (pieces // (2 * bit)) * (2 * bit) * V7X_BF16_ROWS
            rows = bit * V7X_BF16_ROWS
            local = buf_ref.at[slot, pl.ds(pl.multiple_of(seg_start + done, V7X_BF16_ROWS), rows)]
            remote = hbm_ref.at[pl.ds(pl.multiple_of(hbm_start + done, V7X_BF16_ROWS), rows)]
            src, dst = (local, remote) if to_hbm else (remote, local)
            copies.append(((pieces & bit) != 0,
                           pltpu.make_async_copy(src, dst, sems.at[slot, e])))
    return copies


def _start_all(copies):
    for pred, cp in copies:
        pl.when(pred)(cp.start)


def _wait_all(copies):
    for pred, cp in copies:
        pl.when(pred)(cp.wait)


def _moe_dispatch_kernel(off_ref, n16_ref, xn_ref, comb_ref, xs_in_ref, xs_ref, buf_ref, sems):
    del xs_in_ref
    tile = pl.program_id(0)
    slot = tile % 2
    d = xn_ref.shape[1]
    n_slots = buf_ref.shape[1]
    segs = _moe_tile_segments(n16_ref, tile)
    onehot, gate_slot = _moe_tile_slots(comb_ref[...], segs, n_slots)
    buf_ref[slot, :, :d] = _dot(onehot, xn_ref[...]).astype(BF16)
    g_hi = gate_slot.astype(BF16).astype(F32)
    g_lo = gate_slot - g_hi
    lane = lax.broadcasted_iota(jnp.int32, (n_slots, V7X_LANES), 1)
    buf_ref[slot, :, d:] = jnp.where(lane == 0, g_hi,
                                     jnp.where(lane == 1, g_lo, 0.0)).astype(BF16)
    copies = _segment_copies(buf_ref, xs_ref, off_ref, n16_ref, tile, sems, to_hbm=True)
    _start_all(copies)

    @pl.when(tile > 0)
    def _():
        _wait_all(_segment_copies(buf_ref, xs_ref, off_ref, n16_ref, tile - 1, sems, to_hbm=True))

    @pl.when(tile == pl.num_programs(0) - 1)
    def _():
        _wait_all(copies)


def _moe_expert_kernel(te_ref, used_ref, xs_ref, wg_ref, wu_ref, wd_ref, ys_ref):
    d = wg_ref.shape[1]
    live = pl.program_id(0) < used_ref[0]

    @pl.when(live)
    def _():
        xs = xs_ref[:, :d]
        weight = xs_ref[:, d:d + 1].astype(F32) + xs_ref[:, d + 1:d + 2].astype(F32)
        gate = _dot(xs, wg_ref[0])
        up = _dot(xs, wu_ref[0])
        hid = gate * (1.0 / (1.0 + jnp.exp(-gate))) * up * weight
        ys_ref[...] = _dot(hid.astype(BF16), wd_ref[0]).astype(BF16)

    @pl.when(jnp.logical_not(live))
    def _():
        ys_ref[...] = jnp.zeros_like(ys_ref)


def _moe_combine_kernel(off_ref, n16_ref, ys_ref, comb_ref, x_ref, o_ref, buf_ref, sems):
    tile = pl.program_id(0)
    n_slots = buf_ref.shape[1]
    fetch = lambda t: _segment_copies(buf_ref, ys_ref, off_ref, n16_ref, t, sems, to_hbm=False)

    @pl.when(tile == 0)
    def _():
        buf_ref[...] = jnp.zeros_like(buf_ref)
        _start_all(fetch(tile))

    @pl.when(tile + 1 < pl.num_programs(0))
    def _():
        _start_all(fetch(tile + 1))

    onehot, _ = _moe_tile_slots(comb_ref[...], _moe_tile_segments(n16_ref, tile), n_slots)
    _wait_all(fetch(tile))
    o_ref[...] = x_ref[...] + lax.dot_general(onehot, buf_ref[tile % 2], (((0,), (0,)), ((), ())),
                                              preferred_element_type=F32)


def moe_ffn(xn, x, comb, counts, wg, wu, wd, tm=512):
    n, d = x.shape
    n_exp, _, width = wg.shape
    n_tiles = n // tm
    seg_pad = n_exp * (V7X_BF16_ROWS - 1)
    n_slots = -(-(2 * tm + seg_pad) // V7X_LANES) * V7X_LANES
    n_rows = 2 * n + n_tiles * seg_pad + n_exp * (MOE_ROW_TILE - 1)
    n_rows = -(-n_rows // MOE_ROW_TILE) * MOE_ROW_TILE
    tile_off, n16, tile_expert, n_used = _moe_layout(counts, tm, n_rows)
    d_x = d + V7X_LANES
    any_spec = pl.BlockSpec(memory_space=pl.ANY)
    sems = pltpu.SemaphoreType.DMA((2, n_exp))

    xs = pl.pallas_call(
        _moe_dispatch_kernel,
        grid_spec=pltpu.PrefetchScalarGridSpec(
            num_scalar_prefetch=2, grid=(n_tiles,),
            in_specs=[pl.BlockSpec((tm, d), lambda i, *_: (i, 0)),
                      pl.BlockSpec((tm, V7X_LANES), lambda i, *_: (i, 0)), any_spec],
            out_specs=any_spec,
            scratch_shapes=[pltpu.VMEM((2, n_slots, d_x), BF16), sems]),
        out_shape=jax.ShapeDtypeStruct((n_rows, d_x), BF16),
        input_output_aliases={4: 0},
        compiler_params=_params("arbitrary"),
    )(tile_off, n16, xn, comb, jnp.zeros((n_rows, d_x), BF16))

    once = pl.Buffered(1)
    ys = pl.pallas_call(
        _moe_expert_kernel,
        grid_spec=pltpu.PrefetchScalarGridSpec(
            num_scalar_prefetch=2, grid=(n_rows // MOE_ROW_TILE,),
            in_specs=[pl.BlockSpec((MOE_ROW_TILE, d_x), lambda r, te, used: (r, 0)),
                      pl.BlockSpec((1, d, width), lambda r, te, used: (te[r], 0, 0),
                                   pipeline_mode=once),
                      pl.BlockSpec((1, d, width), lambda r, te, used: (te[r], 0, 0),
                                   pipeline_mode=once),
                      pl.BlockSpec((1, width, d), lambda r, te, used: (te[r], 0, 0),
                                   pipeline_mode=once)],
            out_specs=pl.BlockSpec((MOE_ROW_TILE, d), lambda r, te, used: (r, 0))),
        out_shape=jax.ShapeDtypeStruct((n_rows, d), BF16),
        compiler_params=_params("arbitrary"),
    )(tile_expert, n_used, xs, wg, wu, wd)

    return pl.pallas_call(
        _moe_combine_kernel,
        grid_spec=pltpu.PrefetchScalarGridSpec(
            num_scalar_prefetch=2, grid=(n_tiles,),
            in_specs=[any_spec, pl.BlockSpec((tm, V7X_LANES), lambda i, *_: (i, 0)),
                      pl.BlockSpec((tm, d), lambda i, *_: (i, 0))],
            out_specs=pl.BlockSpec((tm, d), lambda i, *_: (i, 0)),
            scratch_shapes=[pltpu.VMEM((2, n_slots, d), BF16), sems]),
        out_shape=jax.ShapeDtypeStruct((n, d), F32),
        compiler_params=_params("arbitrary"),
    )(tile_off, n16, ys, comb, x)


def _slab_gains(q_sb, k_sb, q_dil, k_dil):
    ones = jnp.ones((4, HEAD_DIM), F32)
    rep = lambda g: jnp.broadcast_to(g[None, :], (4, HEAD_DIM))
    return jnp.concatenate([rep(q_sb), rep(k_sb), ones, rep(q_dil), rep(k_dil), ones], axis=0)


def kernel(x, positions, norm_mix, w_in, q_norm_sb, k_norm_sb, q_norm_dil, k_norm_dil, out_norm_sb, out_norm_dil, w_out, norm_ffn, w_gate_dense, w_up_dense, w_down_dense, w_router, w_gate_moe, w_up_moe, w_down_moe):
    batch, seq, d = x.shape
    n = batch * seq
    depth = w_in.shape[0]
    xf = x.reshape(n, d)
    pos_b = jnp.broadcast_to(positions.reshape(n, 1), (n, HEAD_DIM))
    cos, sin = rope_tables(pos_b)
    for i in range(depth):
        gains = _slab_gains(q_norm_sb[i], k_norm_sb[i], q_norm_dil[i], k_norm_dil[i])
        qkv = in_projection(xf, norm_mix[i][None, :], w_in[i].astype(BF16), gains, cos, sin, batch)
        o_sb = stick_breaking_attention(qkv)
        o_dil = dilated_attention(qkv)
        j = i // 2
        moe_layer = i % 2 == 1
        outs = out_projection(
            o_sb, o_dil, xf, out_norm_sb[i][None, :], out_norm_dil[i][None, :],
            w_out[i].astype(BF16), norm_ffn[i][None, :],
            w_router[j].T if moe_layer else None)
        if moe_layer:
            xf, xn, comb, cnt = outs
            counts = cnt[::V7X_SUBLANES, :N_EXPERTS].astype(jnp.int32)
            xf = moe_ffn(xn, xf, comb, counts, w_gate_moe[j].astype(BF16),
                         w_up_moe[j].astype(BF16), w_down_moe[j].astype(BF16))
        else:
            xf, xn = outs
            xf = swiglu_ffn(xn, xf, w_gate_dense[j].astype(BF16), w_up_dense[j].astype(BF16),
                            w_down_dense[j].astype(BF16))
    return xf.reshape(batch, seq, d)
```

```python
import functools
import math

import jax
import jax.numpy as jnp
from jax import lax
from jax.experimental import pallas as pl
from jax.experimental.pallas import tpu as pltpu

HEAD_DIM = 128
N_HEADS_SB = 4
N_HEADS_DIL = 4
N_SLABS = 3 * (N_HEADS_SB + N_HEADS_DIL)
SLAB_Q_SB, SLAB_K_SB, SLAB_V_SB = 0, 4, 8
SLAB_Q_DIL, SLAB_K_DIL, SLAB_V_DIL = 12, 16, 20
DILATED_CONFIGS = ((128, 1), (512, 4), (2048, 16))
DIL_STEPS = 128
N_EXPERTS = 8
ROPE_THETA = 10000.0
EPS = 1e-6
NEG_INF = -1e30
SB_ZERO_WEIGHT_LOG = 112.0

V7X_LANES = 128
V7X_SUBLANES = 8
V7X_BF16_ROWS = 16
V7X_VMEM_LIMIT_BYTES = 56 * 1024 * 1024

F32 = jnp.float32
BF16 = jnp.bfloat16


def _params(*semantics):
    return pltpu.CompilerParams(dimension_semantics=semantics,
                                vmem_limit_bytes=V7X_VMEM_LIMIT_BYTES)


def _dot(a, b):
    return jnp.dot(a, b, preferred_element_type=F32)


def _dot_nt(a, b):
    return lax.dot_general(a, b, (((1,), (1,)), ((), ())), preferred_element_type=F32)


def _rope_kernel(pos_ref, cos_ref, sin_ref):
    lane = lax.broadcasted_iota(jnp.int32, pos_ref.shape, 1)
    half = HEAD_DIM // 2
    pair = (lane % half).astype(F32)
    inv = jnp.exp(pair * (-2.0 * math.log(ROPE_THETA) / HEAD_DIM))
    ang = pos_ref[...].astype(F32) * inv
    cos_ref[...] = jnp.cos(ang)
    s = jnp.sin(ang)
    sin_ref[...] = jnp.where(lane < half, -s, s)


def rope_tables(pos_b, tm=1024):
    n = pos_b.shape[0]
    spec = pl.BlockSpec((tm, HEAD_DIM), lambda i: (i, 0))
    return pl.pallas_call(
        _rope_kernel,
        grid=(n // tm,),
        in_specs=[spec],
        out_specs=[spec, spec],
        out_shape=[jax.ShapeDtypeStruct((n, HEAD_DIM), F32)] * 2,
        compiler_params=_params("parallel"),
    )(pos_b)


def _inproj_kernel(x_ref, g_ref, w_ref, gains_ref, cos_ref, sin_ref, o_ref):
    x = x_ref[...]
    ms = jnp.mean(x * x, axis=-1, keepdims=True)
    xn = (x * lax.rsqrt(ms + EPS) * g_ref[...]).astype(BF16)
    cos = cos_ref[...]
    sin = sin_ref[...]
    scale = 1.0 / math.sqrt(HEAD_DIM)
    group = 4
    for j0 in range(0, N_SLABS, group):
        h = _dot(xn, w_ref[:, j0 * HEAD_DIM:(j0 + group) * HEAD_DIM])
        for j in range(j0, j0 + group):
            y = h[:, (j - j0) * HEAD_DIM:(j - j0 + 1) * HEAD_DIM]
            is_v = SLAB_V_SB <= j < SLAB_Q_DIL or j >= SLAB_V_DIL
            if not is_v:
                ms_h = jnp.mean(y * y, axis=-1, keepdims=True)
                y = y * lax.rsqrt(ms_h + EPS) * gains_ref[j:j + 1, :]
                if j >= SLAB_Q_DIL:
                    y = y * cos + pltpu.roll(y, HEAD_DIM // 2, 1) * sin
                if j < SLAB_K_SB or SLAB_Q_DIL <= j < SLAB_K_DIL:
                    y = y * scale
            o_ref[0, j] = y.astype(BF16)


def in_projection(x, g_norm, w_in, gains, cos, sin, batch, tm=512):
    n, d = x.shape
    s = n // batch
    d_in = w_in.shape[1]
    nt = s // tm
    return pl.pallas_call(
        _inproj_kernel,
        grid=(batch, nt),
        in_specs=[
            pl.BlockSpec((tm, d), lambda b, i: (b * nt + i, 0)),
            pl.BlockSpec((1, d), lambda b, i: (0, 0)),
            pl.BlockSpec((d, d_in), lambda b, i: (0, 0)),
            pl.BlockSpec((N_SLABS, HEAD_DIM), lambda b, i: (0, 0)),
            pl.BlockSpec((tm, HEAD_DIM), lambda b, i: (b * nt + i, 0)),
            pl.BlockSpec((tm, HEAD_DIM), lambda b, i: (b * nt + i, 0)),
        ],
        out_specs=pl.BlockSpec((1, N_SLABS, tm, HEAD_DIM), lambda b, i: (b, 0, i, 0)),
        out_shape=jax.ShapeDtypeStruct((batch, N_SLABS, s, HEAD_DIM), BF16),
        compiler_params=_params("parallel", "parallel"),
    )(x, g_norm, w_in, gains, cos, sin)


def _softplus(z):
    return jnp.maximum(z, 0.0) + jnp.log(1.0 + jnp.exp(-jnp.abs(z)))


def _sb_kernel(q_ref, k_ref, v_ref, o_ref, *, blk, nsub):
    i = pl.program_id(2)
    row = lax.broadcasted_iota(jnp.int32, (blk, blk), 0)
    col = lax.broadcasted_iota(jnp.int32, (blk, blk), 1)
    incl = (row >= col).astype(BF16)
    strict = col < row

    def block(q, c, later, acc, diagonal):
        start = pl.multiple_of(jnp.maximum(c, 0) * blk, blk)
        kb = k_ref[0, 0, pl.ds(start, blk), :]
        vb = v_ref[0, 0, pl.ds(start, blk), :]
        z = _dot_nt(q, kb)
        sp = _softplus(z)
        if diagonal:
            sp = jnp.where(strict, sp, 0.0)
        sp_hi = sp.astype(BF16)
        sp_lo = (sp - sp_hi.astype(F32)).astype(BF16)
        r = _dot(sp_hi, incl) + _dot(sp_lo, incl)
        a = jnp.exp(z - r - later)
        total = r[:, :1]
        if diagonal:
            a = jnp.where(strict, a, 0.0)
        else:
            a = jnp.where(c >= 0, a, 0.0)
            total = jnp.where(c >= 0, total, 0.0)
        return later + total, acc + _dot(a.astype(BF16), vb)

    qs = [q_ref[0, 0, j * blk:(j + 1) * blk, :] for j in range(nsub)]
    own = [i * nsub + j for j in range(nsub)]
    state = [block(qs[j], own[j], jnp.zeros((blk, 1), F32), jnp.zeros((blk, HEAD_DIM), F32), True)
             for j in range(nsub)]

    def unfinished(back, laters):
        flags = [jnp.logical_and(own[j] - back >= 0, jnp.min(laters[j]) < SB_ZERO_WEIGHT_LOG)
                 for j in range(nsub)]
        return functools.reduce(jnp.logical_or, flags)

    def cond(carry):
        return carry[1]

    def body(carry):
        back, _, laters, accs = carry
        new = [block(qs[j], own[j] - back, laters[j], accs[j], False) for j in range(nsub)]
        laters, accs = [s[0] for s in new], [s[1] for s in new]
        return back + 1, unfinished(back + 1, laters), laters, accs

    state = [block(qs[j], own[j] - 1, state[j][0], state[j][1], False) for j in range(nsub)]
    laters, accs = [s[0] for s in state], [s[1] for s in state]
    _, _, _, accs = lax.while_loop(cond, body, (2, unfinished(2, laters), laters, accs))
    for j in range(nsub):
        o_ref[0, j * blk:(j + 1) * blk, :] = accs[j].astype(o_ref.dtype)


def stick_breaking_attention(qkv, blk=256, nsub=4):
    batch, _, s, dh = qkv.shape
    nh = N_HEADS_SB
    o = pl.pallas_call(
        functools.partial(_sb_kernel, blk=blk, nsub=nsub),
        grid=(batch, nh, s // (blk * nsub)),
        in_specs=[
            pl.BlockSpec((1, 1, blk * nsub, dh), lambda b, h, i: (b, SLAB_Q_SB + h, i, 0)),
            pl.BlockSpec((1, 1, s, dh), lambda b, h, i: (b, SLAB_K_SB + h, 0, 0)),
            pl.BlockSpec((1, 1, s, dh), lambda b, h, i: (b, SLAB_V_SB + h, 0, 0)),
        ],
        out_specs=pl.BlockSpec((1, blk * nsub, dh), lambda b, h, i: (b, i, h)),
        out_shape=jax.ShapeDtypeStruct((batch, s, nh * dh), BF16),
        compiler_params=_params("parallel", "parallel", "arbitrary"),
    )(qkv, qkv, qkv)
    return o.reshape(batch * s, nh * dh)


def _dil_kernel(q_ref, k_ref, v_ref, o_ref, kf_ref, vf_ref, qf_ref, po_ref, pm_ref, pl_ref):
    i = pl.program_id(2)
    st = DIL_STEPS
    span, dh = qf_ref.shape
    seq = kf_ref.shape[0]
    chunk = 512

    @pl.when(i == 0)
    def _():
        def conv(c, carry):
            rows = pl.ds(pl.multiple_of(c * chunk, chunk), chunk)
            kf_ref[rows, :] = k_ref[0, 0, rows, :].astype(F32)
            vf_ref[rows, :] = v_ref[0, 0, rows, :].astype(F32)
            return carry
        lax.fori_loop(0, seq // chunk, conv, 0)

    qf_ref[...] = q_ref[0, 0].astype(F32)

    diff = (lax.broadcasted_iota(jnp.int32, (st, 2 * st), 1)
            - lax.broadcasted_iota(jnp.int32, (st, 2 * st), 0))
    bias_normal = jnp.where((diff >= 0) & (diff <= st), 0.0, NEG_INF)
    bias_first = jnp.where(diff <= 0, 0.0, NEG_INF)
    ones = jnp.ones((2 * st, dh), BF16)

    def strided(start, size, dil):
        return pl.ds(start, size) if dil == 1 else pl.ds(start, size, stride=dil)

    for c, (_, dil) in enumerate(DILATED_CONFIGS):
        blk_rows = st * dil

        def unit(u, carry, c=c, dil=dil, blk_rows=blk_rows):
            r = u % dil
            q_start = (u // dil) * blk_rows + r
            p0 = i * span + (u // dil) * blk_rows
            first = p0 == 0
            w_start = jnp.where(first, r, p0 - blk_rows + r)
            q = qf_ref[strided(q_start, st, dil), :].astype(BF16)
            kw = kf_ref[strided(w_start, 2 * st, dil), :].astype(BF16)
            vw = vf_ref[strided(w_start, 2 * st, dil), :].astype(BF16)
            s = _dot_nt(q, kw) + jnp.where(first, bias_first, bias_normal)
            m = jnp.max(s, axis=-1, keepdims=True)
            p = jnp.exp(s - m).astype(BF16)
            o_den = _dot(p, jnp.concatenate([vw, ones], axis=1))
            rows = strided(q_start, st, dil)
            po_ref[c, rows, :] = o_den[:, :dh]
            pl_ref[c, rows, :] = o_den[:, dh:]
            pm_ref[c, rows, :] = jnp.broadcast_to(m, (st, dh))
            return carry

        lax.fori_loop(0, span // st, unit, 0, unroll=8)

    def merge(c, carry):
        rows = pl.ds(pl.multiple_of(c * st, st), st)
        ms = [pm_ref[g, rows, :] for g in range(len(DILATED_CONFIGS))]
        m = functools.reduce(jnp.maximum, ms)
        ws = [jnp.exp(mg - m) for mg in ms]
        den = sum(w * pl_ref[g, rows, :] for g, w in enumerate(ws))
        num = sum(w * po_ref[g, rows, :] for g, w in enumerate(ws))
        o_ref[0, rows, :] = (num / den).astype(o_ref.dtype)
        return carry

    lax.fori_loop(0, span // st, merge, 0, unroll=2)


def dilated_attention(qkv):
    batch, _, s, dh = qkv.shape
    nh = N_HEADS_DIL
    span = DIL_STEPS * max(dil for _, dil in DILATED_CONFIGS)
    assert s % span == 0 and s >= 2 * span
    n_cfg = len(DILATED_CONFIGS)
    o = pl.pallas_call(
        _dil_kernel,
        grid=(batch, nh, s // span),
        in_specs=[
            pl.BlockSpec((1, 1, span, dh), lambda b, h, i: (b, SLAB_Q_DIL + h, i, 0)),
            pl.BlockSpec((1, 1, s, dh), lambda b, h, i: (b, SLAB_K_DIL + h, 0, 0)),
            pl.BlockSpec((1, 1, s, dh), lambda b, h, i: (b, SLAB_V_DIL + h, 0, 0)),
        ],
        out_specs=pl.BlockSpec((1, span, dh), lambda b, h, i: (b, i, h)),
        out_shape=jax.ShapeDtypeStruct((batch, s, nh * dh), BF16),
        scratch_shapes=[
            pltpu.VMEM((s, dh), F32), pltpu.VMEM((s, dh), F32), pltpu.VMEM((span, dh), F32),
            pltpu.VMEM((n_cfg, span, dh), F32), pltpu.VMEM((n_cfg, span, dh), F32),
            pltpu.VMEM((n_cfg, span, dh), F32)],
        compiler_params=_params("parallel", "parallel", "arbitrary"),
    )(qkv, qkv, qkv)
    return o.reshape(batch * s, nh * dh)


def _rms(y, g):
    return y * lax.rsqrt(jnp.mean(y * y, axis=-1, keepdims=True) + EPS) * g


def _mix_out(osb_ref, odil_ref, x_ref, gsb_ref, gdil_ref, w_ref, gffn_ref):
    d_sb = osb_ref.shape[1]
    y_sb = _rms(osb_ref[...].astype(F32), gsb_ref[...]).astype(BF16)
    y_dil = _rms(odil_ref[...].astype(F32), gdil_ref[...]).astype(BF16)
    x = x_ref[...] + _dot(y_sb, w_ref[:d_sb, :]) + _dot(y_dil, w_ref[d_sb:, :])
    return x, _rms(x, gffn_ref[...])


def _outproj_router_kernel(osb_ref, odil_ref, x_ref, gsb_ref, gdil_ref, w_ref, gffn_ref, wr_ref,
                           xo_ref, xn_ref, comb_ref, cnt_ref):
    x, xn = _mix_out(osb_ref, odil_ref, x_ref, gsb_ref, gdil_ref, w_ref, gffn_ref)
    xo_ref[...] = x
    xn_ref[...] = xn.astype(BF16)

    lane = lax.broadcasted_iota(jnp.int32, comb_ref.shape, 1)
    logits = jnp.full(comb_ref.shape, -jnp.inf, F32)
    for e in range(N_EXPERTS):
        le = jnp.sum(xn * wr_ref[e:e + 1, :], axis=-1, keepdims=True)
        logits = jnp.where(lane == e, le, logits)
    top1 = jnp.max(logits, axis=-1, keepdims=True)
    idx1 = jnp.min(jnp.where(logits == top1, lane, V7X_LANES), axis=-1, keepdims=True)
    rest = jnp.where(lane == idx1, -jnp.inf, logits)
    top2 = jnp.max(rest, axis=-1, keepdims=True)
    idx2 = jnp.min(jnp.where(rest == top2, lane, V7X_LANES), axis=-1, keepdims=True)
    t = jnp.exp(top2 - top1)
    gate1 = 1.0 / (1.0 + t)
    gate2 = t / (1.0 + t)
    comb = jnp.where(lane == idx1, gate1, 0.0) + jnp.where(lane == idx2, gate2, 0.0)
    comb_ref[...] = comb
    routed = jnp.sum(jnp.where(comb > 0.0, 1.0, 0.0), axis=0, keepdims=True)
    cnt_ref[...] = jnp.broadcast_to(routed, cnt_ref.shape)


def _mix_specs(tm, d, d_sb, d_dil, index):
    row = lambda width: pl.BlockSpec((tm, width), lambda *g: (index(*g), 0))
    const = lambda shape: pl.BlockSpec(shape, lambda *g: (0, 0))
    return [row(d_sb), row(d_dil), row(d), const((1, d_sb)), const((1, d_dil)),
            const((d_sb + d_dil, d)), const((1, d))]


def out_projection_router(o_sb, o_dil, x, g_sb, g_dil, w_out, g_ffn, w_router_t, tm=512):
    n, d = x.shape
    row = lambda width: pl.BlockSpec((tm, width), lambda i: (i, 0))
    return pl.pallas_call(
        _outproj_router_kernel,
        grid=(n // tm,),
        in_specs=_mix_specs(tm, d, o_sb.shape[1], o_dil.shape[1], lambda i: i) + [
            pl.BlockSpec(w_router_t.shape, lambda i: (0, 0))],
        out_specs=[row(d), row(d), row(V7X_LANES),
                   pl.BlockSpec((V7X_SUBLANES, V7X_LANES), lambda i: (i, 0))],
        out_shape=[jax.ShapeDtypeStruct((n, d), F32), jax.ShapeDtypeStruct((n, d), BF16),
                   jax.ShapeDtypeStruct((n, V7X_LANES), F32),
                   jax.ShapeDtypeStruct((n // tm * V7X_SUBLANES, V7X_LANES), F32)],
        compiler_params=_params("parallel"),
    )(o_sb, o_dil, x, g_sb, g_dil, w_out, g_ffn, w_router_t)


def _outproj_ffn_kernel(osb_ref, odil_ref, x_ref, gsb_ref, gdil_ref, w_ref, gffn_ref,
                        wg_ref, wu_ref, wd_ref, o_ref, xn_ref):
    @pl.when(pl.program_id(1) == 0)
    def _():
        x, xn = _mix_out(osb_ref, odil_ref, x_ref, gsb_ref, gdil_ref, w_ref, gffn_ref)
        o_ref[...] = x
        xn_ref[...] = xn.astype(BF16)

    xn = xn_ref[...]
    gate = _dot(xn, wg_ref[...])
    up = _dot(xn, wu_ref[...])
    hid = gate * (1.0 / (1.0 + jnp.exp(-gate))) * up
    o_ref[...] += _dot(hid.astype(BF16), wd_ref[...])


def out_projection_ffn(o_sb, o_dil, x, g_sb, g_dil, w_out, g_ffn, wg, wu, wd, tm=512, tf=512):
    n, d = x.shape
    width = wg.shape[1]
    assert width % tf == 0
    return pl.pallas_call(
        _outproj_ffn_kernel,
        grid=(n // tm, width // tf),
        in_specs=_mix_specs(tm, d, o_sb.shape[1], o_dil.shape[1], lambda i, f: i) + [
            pl.BlockSpec((d, tf), lambda i, f: (0, f)),
            pl.BlockSpec((d, tf), lambda i, f: (0, f)),
            pl.BlockSpec((tf, d), lambda i, f: (f, 0))],
        out_specs=pl.BlockSpec((tm, d), lambda i, f: (i, 0)),
        out_shape=jax.ShapeDtypeStruct((n, d), F32),
        scratch_shapes=[pltpu.VMEM((tm, d), BF16)],
        compiler_params=_params("parallel", "arbitrary"),
    )(o_sb, o_dil, x, g_sb, g_dil, w_out, g_ffn, wg, wu, wd)


MOE_ROW_TILE = 512
MOE_SEG_BITS = (32, 16, 8, 4, 2, 1)


def _moe_layout(counts, tm, n_rows):
    n_exp = counts.shape[1]
    n16 = (counts + V7X_BF16_ROWS - 1) // V7X_BF16_ROWS
    rows = n16 * V7X_BF16_ROWS
    region = jnp.sum(rows, axis=0)
    region = (region + MOE_ROW_TILE - 1) // MOE_ROW_TILE * MOE_ROW_TILE
    ends = jnp.cumsum(region)
    tile_off = (ends - region)[None, :] + jnp.cumsum(rows, axis=0) - rows
    row_start = jnp.arange(n_rows // MOE_ROW_TILE, dtype=jnp.int32) * MOE_ROW_TILE
    tile_expert = jnp.minimum(jnp.sum(row_start[:, None] >= ends[None, :], axis=1), n_exp - 1)
    n_used = ends[-1:] // MOE_ROW_TILE
    i32 = lambda a: a.astype(jnp.int32)
    return i32(tile_off).reshape(-1), i32(n16).reshape(-1), i32(tile_expert), i32(n_used)


def _moe_tile_segments(n16_ref, tile):
    segs, start = [], jnp.int32(0)
    for e in range(N_EXPERTS):
        pieces = n16_ref[tile * N_EXPERTS + e]
        segs.append((start, pieces))
        start = start + pieces * V7X_BF16_ROWS
    return segs


def _moe_tile_slots(comb, segs, n_slots):
    tm = comb.shape[0]
    rows_e = V7X_BF16_ROWS
    comb_t = comb.T[:rows_e, :]
    routed = comb_t > 0.0
    before = (lax.broadcasted_iota(jnp.int32, (tm, tm), 0)
              < lax.broadcasted_iota(jnp.int32, (tm, tm), 1))
    rank = _dot(jnp.where(routed, 1.0, 0.0).astype(BF16),
                jnp.where(before, 1.0, 0.0).astype(BF16))
    sub = lax.broadcasted_iota(jnp.int32, (rows_e, tm), 0)
    seg_col = jnp.zeros((rows_e, tm), F32)
    for e, (start, _) in enumerate(segs):
        seg_col = jnp.where(sub == e, start.astype(F32), seg_col)
    slot = jnp.where(routed, rank + seg_col, -1.0)
    n_routed = jnp.sum(jnp.where(routed, 1.0, 0.0), axis=0, keepdims=True)
    slot_hi = jnp.max(slot, axis=0, keepdims=True)
    slot_lo = jnp.where(n_routed >= 2.0,
                        jnp.sum(jnp.where(routed, slot, 0.0), axis=0, keepdims=True) - slot_hi,
                        -1.0)
    gate_hi = jnp.sum(jnp.where(slot == slot_hi, comb_t, 0.0), axis=0, keepdims=True)
    gate_lo = jnp.sum(comb_t, axis=0, keepdims=True) - gate_hi
    s_idx = lax.broadcasted_iota(jnp.int32, (n_slots, tm), 0).astype(F32)
    hit_hi = s_idx == slot_hi
    hit_lo = s_idx == slot_lo
    onehot = jnp.where(hit_hi, 1.0, jnp.where(hit_lo, 1.0, 0.0)).astype(BF16)
    gate_slot = jnp.sum(jnp.where(hit_hi, gate_hi, jnp.where(hit_lo, gate_lo, 0.0)),
                        axis=1, keepdims=True)
    return onehot, gate_slot


def _segment_copies(buf_ref, hbm_ref, off_ref, n16_ref, tile, sems, to_hbm):
    slot = tile % 2
    copies = []
    for e, (seg_start, pieces) in enumerate(_moe_tile_segments(n16_ref, tile)):
        hbm_start = off_ref[tile * N_EXPERTS + e]
        for bit in MOE_SEG_BITS:
            done = (pieces // (2 * bit)) * (2 * bit) * V7X_BF16_ROWS
            rows = bit * V7X_BF16_ROWS
            local = buf_ref.at[slot, pl.ds(pl.multiple_of(seg_start + done, V7X_BF16_ROWS), rows)]
            remote = hbm_ref.at[pl.ds(pl.multiple_of(hbm_start + done, V7X_BF16_ROWS), rows)]
            src, dst = (local, remote) if to_hbm else (remote, local)
            copies.append(((pieces & bit) != 0,
                           pltpu.make_async_copy(src, dst, sems.at[slot, e])))
    return copies


def _start_all(copies):
    for pred, cp in copies:
        pl.when(pred)(cp.start)


def _wait_all(copies):
    for pred, cp in copies:
        pl.when(pred)(cp.wait)


def _moe_dispatch_kernel(off_ref, n16_ref, xn_ref, comb_ref, xs_in_ref, xs_ref, buf_ref, sems):
    del xs_in_ref
    tile = pl.program_id(0)
    slot = tile % 2
    d = xn_ref.shape[1]
    n_slots = buf_ref.shape[1]
    segs = _moe_tile_segments(n16_ref, tile)
    onehot, gate_slot = _moe_tile_slots(comb_ref[...], segs, n_slots)
    buf_ref[slot, :, :d] = _dot(onehot, xn_ref[...]).astype(BF16)
    g_hi = gate_slot.astype(BF16).astype(F32)
    g_lo = gate_slot - g_hi
    lane = lax.broadcasted_iota(jnp.int32, (n_slots, V7X_LANES), 1)
    buf_ref[slot, :, d:] = jnp.where(lane == 0, g_hi,
                                     jnp.where(lane == 1, g_lo, 0.0)).astype(BF16)
    copies = _segment_copies(buf_ref, xs_ref, off_ref, n16_ref, tile, sems, to_hbm=True)
    _start_all(copies)

    @pl.when(tile > 0)
    def _():
        _wait_all(_segment_copies(buf_ref, xs_ref, off_ref, n16_ref, tile - 1, sems, to_hbm=True))

    @pl.when(tile == pl.num_programs(0) - 1)
    def _():
        _wait_all(copies)


def _moe_expert_kernel(te_ref, used_ref, xs_ref, wg_ref, wu_ref, wd_ref, ys_ref):
    d = wg_ref.shape[1]
    live = pl.program_id(0) < used_ref[0]

    @pl.when(live)
    def _():
        xs = xs_ref[:, :d]
        weight = xs_ref[:, d:d + 1].astype(F32) + xs_ref[:, d + 1:d + 2].astype(F32)
        gate = _dot(xs, wg_ref[0])
        up = _dot(xs, wu_ref[0])
        hid = gate * (1.0 / (1.0 + jnp.exp(-gate))) * up * weight
        ys_ref[...] = _dot(hid.astype(BF16), wd_ref[0]).astype(BF16)

    @pl.when(jnp.logical_not(live))
    def _():
        ys_ref[...] = jnp.zeros_like(ys_ref)


def _moe_combine_kernel(off_ref, n16_ref, ys_ref, comb_ref, x_ref, o_ref, buf_ref, sems):
    tile = pl.program_id(0)
    n_slots = buf_ref.shape[1]
    fetch = lambda t: _segment_copies(buf_ref, ys_ref, off_ref, n16_ref, t, sems, to_hbm=False)

    @pl.when(tile == 0)
    def _():
        buf_ref[...] = jnp.zeros_like(buf_ref)
        _start_all(fetch(tile))

    @pl.when(tile + 1 < pl.num_programs(0))
    def _():
        _start_all(fetch(tile + 1))

    onehot, _ = _moe_tile_slots(comb_ref[...], _moe_tile_segments(n16_ref, tile), n_slots)
    _wait_all(fetch(tile))
    o_ref[...] = x_ref[...] + lax.dot_general(onehot, buf_ref[tile % 2], (((0,), (0,)), ((), ())),
                                              preferred_element_type=F32)


def moe_ffn(xn, x, comb, counts, wg, wu, wd, tm=512):
    n, d = x.shape
    n_exp, _, width = wg.shape
    n_tiles = n // tm
    seg_pad = n_exp * (V7X_BF16_ROWS - 1)
    n_slots = -(-(2 * tm + seg_pad) // V7X_LANES) * V7X_LANES
    n_rows = 2 * n + n_tiles * seg_pad + n_exp * (MOE_ROW_TILE - 1)
    n_rows = -(-n_rows // MOE_ROW_TILE) * MOE_ROW_TILE
    tile_off, n16, tile_expert, n_used = _moe_layout(counts, tm, n_rows)
    d_x = d + V7X_LANES
    any_spec = pl.BlockSpec(memory_space=pl.ANY)
    sems = pltpu.SemaphoreType.DMA((2, n_exp))

    xs = pl.pallas_call(
        _moe_dispatch_kernel,
        grid_spec=pltpu.PrefetchScalarGridSpec(
            num_scalar_prefetch=2, grid=(n_tiles,),
            in_specs=[pl.BlockSpec((tm, d), lambda i, *_: (i, 0)),
                      pl.BlockSpec((tm, V7X_LANES), lambda i, *_: (i, 0)), any_spec],
            out_specs=any_spec,
            scratch_shapes=[pltpu.VMEM((2, n_slots, d_x), BF16), sems]),
        out_shape=jax.ShapeDtypeStruct((n_rows, d_x), BF16),
        input_output_aliases={4: 0},
        compiler_params=_params("arbitrary"),
    )(tile_off, n16, xn, comb, jnp.zeros((n_rows, d_x), BF16))

    once = pl.Buffered(1)
    ys = pl.pallas_call(
        _moe_expert_kernel,
        grid_spec=pltpu.PrefetchScalarGridSpec(
            num_scalar_prefetch=2, grid=(n_rows // MOE_ROW_TILE,),
            in_specs=[pl.BlockSpec((MOE_ROW_TILE, d_x), lambda r, te, used: (r, 0)),
                      pl.BlockSpec((1, d, width), lambda r, te, used: (te[r], 0, 0),
                                   pipeline_mode=once),
                      pl.BlockSpec((1, d, width), lambda r, te, used: (te[r], 0, 0),
                                   pipeline_mode=once),
                      pl.BlockSpec((1, width, d), lambda r, te, used: (te[r], 0, 0),
                                   pipeline_mode=once)],
            out_specs=pl.BlockSpec((MOE_ROW_TILE, d), lambda r, te, used: (r, 0))),
        out_shape=jax.ShapeDtypeStruct((n_rows, d), BF16),
        compiler_params=_params("arbitrary"),
    )(tile_expert, n_used, xs, wg, wu, wd)

    return pl.pallas_call(
        _moe_combine_kernel,
        grid_spec=pltpu.PrefetchScalarGridSpec(
            num_scalar_prefetch=2, grid=(n_tiles,),
            in_specs=[any_spec, pl.BlockSpec((tm, V7X_LANES), lambda i, *_: (i, 0)),
                      pl.BlockSpec((tm, d), lambda i, *_: (i, 0))],
            out_specs=pl.BlockSpec((tm, d), lambda i, *_: (i, 0)),
            scratch_shapes=[pltpu.VMEM((2, n_slots, d), BF16), sems]),
        out_shape=jax.ShapeDtypeStruct((n, d), F32),
        compiler_params=_params("arbitrary"),
    )(tile_off, n16, ys, comb, x)


def _slab_gains(q_sb, k_sb, q_dil, k_dil):
    ones = jnp.ones((4, HEAD_DIM), F32)
    rep = lambda g: jnp.broadcast_to(g[None, :], (4, HEAD_DIM))
    return jnp.concatenate([rep(q_sb), rep(k_sb), ones, rep(q_dil), rep(k_dil), ones], axis=0)


def kernel(x, positions, norm_mix, w_in, q_norm_sb, k_norm_sb, q_norm_dil, k_norm_dil, out_norm_sb, out_norm_dil, w_out, norm_ffn, w_gate_dense, w_up_dense, w_down_dense, w_router, w_gate_moe, w_up_moe, w_down_moe):
    batch, seq, d = x.shape
    n = batch * seq
    depth = w_in.shape[0]
    xf = x.reshape(n, d)
    pos_b = jnp.broadcast_to(positions.reshape(n, 1), (n, HEAD_DIM))
    cos, sin = rope_tables(pos_b)
    for i in range(depth):
        gains = _slab_gains(q_norm_sb[i], k_norm_sb[i], q_norm_dil[i], k_norm_dil[i])
        qkv = in_projection(xf, norm_mix[i][None, :], w_in[i].astype(BF16), gains, cos, sin, batch)
        o_sb = stick_breaking_attention(qkv)
        o_dil = dilated_attention(qkv)
        j = i // 2
        mix = (o_sb, o_dil, xf, out_norm_sb[i][None, :], out_norm_dil[i][None, :],
               w_out[i].astype(BF16), norm_ffn[i][None, :])
        if i % 2 == 1:
            xf, xn, comb, cnt = out_projection_router(*mix, w_router[j].T)
            counts = cnt[::V7X_SUBLANES, :N_EXPERTS].astype(jnp.int32)
            xf = moe_ffn(xn, xf, comb, counts, w_gate_moe[j].astype(BF16),
                         w_up_moe[j].astype(BF16), w_down_moe[j].astype(BF16))
        else:
            xf = out_projection_ffn(*mix, w_gate_dense[j].astype(BF16),
                                    w_up_dense[j].astype(BF16), w_down_dense[j].astype(BF16))
    return xf.reshape(batch, seq, d)
```

```python
import functools
import math

import jax
import jax.numpy as jnp
from jax import lax
from jax.experimental import pallas as pl
from jax.experimental.pallas import tpu as pltpu

HEAD_DIM = 128
N_HEADS_SB = 4
N_HEADS_DIL = 4
N_SLABS = 3 * (N_HEADS_SB + N_HEADS_DIL)
SLAB_Q_SB, SLAB_K_SB, SLAB_V_SB = 0, 4, 8
SLAB_Q_DIL, SLAB_K_DIL, SLAB_V_DIL = 12, 16, 20
DILATED_CONFIGS = ((128, 1), (512, 4), (2048, 16))
DIL_STEPS = 128
N_EXPERTS = 8
ROPE_THETA = 10000.0
EPS = 1e-6
NEG_INF = -1e30
SB_ZERO_WEIGHT_LOG = 112.0

V7X_LANES = 128
V7X_SUBLANES = 8
V7X_BF16_ROWS = 16
V7X_VMEM_LIMIT_BYTES = 56 * 1024 * 1024

F32 = jnp.float32
BF16 = jnp.bfloat16


def _params(*semantics):
    return pltpu.CompilerParams(dimension_semantics=semantics,
                                vmem_limit_bytes=V7X_VMEM_LIMIT_BYTES)


def _dot(a, b):
    return jnp.dot(a, b, preferred_element_type=F32)


def _dot_nt(a, b):
    return lax.dot_general(a, b, (((1,), (1,)), ((), ())), preferred_element_type=F32)


def _rope_kernel(pos_ref, cos_ref, sin_ref):
    lane = lax.broadcasted_iota(jnp.int32, pos_ref.shape, 1)
    half = HEAD_DIM // 2
    pair = (lane % half).astype(F32)
    inv = jnp.exp(pair * (-2.0 * math.log(ROPE_THETA) / HEAD_DIM))
    ang = pos_ref[...].astype(F32) * inv
    cos_ref[...] = jnp.cos(ang)
    s = jnp.sin(ang)
    sin_ref[...] = jnp.where(lane < half, -s, s)


def rope_tables(pos_b, tm=1024):
    n = pos_b.shape[0]
    spec = pl.BlockSpec((tm, HEAD_DIM), lambda i: (i, 0))
    return pl.pallas_call(
        _rope_kernel,
        grid=(n // tm,),
        in_specs=[spec],
        out_specs=[spec, spec],
        out_shape=[jax.ShapeDtypeStruct((n, HEAD_DIM), F32)] * 2,
        compiler_params=_params("parallel"),
    )(pos_b)


def _inproj_kernel(x_ref, g_ref, w_ref, gains_ref, cos_ref, sin_ref, o_ref):
    x = x_ref[...]
    ms = jnp.mean(x * x, axis=-1, keepdims=True)
    xn = (x * lax.rsqrt(ms + EPS) * g_ref[...]).astype(BF16)
    cos = cos_ref[...]
    sin = sin_ref[...]
    scale = 1.0 / math.sqrt(HEAD_DIM)
    group = 4
    for j0 in range(0, N_SLABS, group):
        h = _dot(xn, w_ref[:, j0 * HEAD_DIM:(j0 + group) * HEAD_DIM])
        for j in range(j0, j0 + group):
            y = h[:, (j - j0) * HEAD_DIM:(j - j0 + 1) * HEAD_DIM]
            is_v = SLAB_V_SB <= j < SLAB_Q_DIL or j >= SLAB_V_DIL
            if not is_v:
                ms_h = jnp.mean(y * y, axis=-1, keepdims=True)
                y = y * lax.rsqrt(ms_h + EPS) * gains_ref[j:j + 1, :]
                if j >= SLAB_Q_DIL:
                    y = y * cos + pltpu.roll(y, HEAD_DIM // 2, 1) * sin
                if j < SLAB_K_SB or SLAB_Q_DIL <= j < SLAB_K_DIL:
                    y = y * scale
            o_ref[0, j] = y.astype(BF16)


def in_projection(x, g_norm, w_in, gains, cos, sin, batch, tm=512):
    n, d = x.shape
    s = n // batch
    d_in = w_in.shape[1]
    nt = s // tm
    return pl.pallas_call(
        _inproj_kernel,
        grid=(batch, nt),
        in_specs=[
            pl.BlockSpec((tm, d), lambda b, i: (b * nt + i, 0)),
            pl.BlockSpec((1, d), lambda b, i: (0, 0)),
            pl.BlockSpec((d, d_in), lambda b, i: (0, 0)),
            pl.BlockSpec((N_SLABS, HEAD_DIM), lambda b, i: (0, 0)),
            pl.BlockSpec((tm, HEAD_DIM), lambda b, i: (b * nt + i, 0)),
            pl.BlockSpec((tm, HEAD_DIM), lambda b, i: (b * nt + i, 0)),
        ],
        out_specs=pl.BlockSpec((1, N_SLABS, tm, HEAD_DIM), lambda b, i: (b, 0, i, 0)),
        out_shape=jax.ShapeDtypeStruct((batch, N_SLABS, s, HEAD_DIM), BF16),
        compiler_params=_params("parallel", "parallel"),
    )(x, g_norm, w_in, gains, cos, sin)


def _softplus(z):
    return jnp.maximum(z, 0.0) + jnp.log(1.0 + jnp.exp(-jnp.abs(z)))


def _sb_kernel(q_ref, k_ref, v_ref, o_ref, *, blk, nsub):
    i = pl.program_id(2)
    row = lax.broadcasted_iota(jnp.int32, (blk, blk), 0)
    col = lax.broadcasted_iota(jnp.int32, (blk, blk), 1)
    incl = (row >= col).astype(BF16)
    strict = col < row

    def block(q, c, later, acc, diagonal):
        start = pl.multiple_of(jnp.maximum(c, 0) * blk, blk)
        kb = k_ref[0, 0, pl.ds(start, blk), :]
        vb = v_ref[0, 0, pl.ds(start, blk), :]
        z = _dot_nt(q, kb)
        sp = _softplus(z)
        if diagonal:
            sp = jnp.where(strict, sp, 0.0)
        sp_hi = sp.astype(BF16)
        sp_lo = (sp - sp_hi.astype(F32)).astype(BF16)
        r = _dot(sp_hi, incl) + _dot(sp_lo, incl)
        a = jnp.exp(z - r - later)
        total = r[:, :1]
        if diagonal:
            a = jnp.where(strict, a, 0.0)
        else:
            a = jnp.where(c >= 0, a, 0.0)
            total = jnp.where(c >= 0, total, 0.0)
        return later + total, acc + _dot(a.astype(BF16), vb)

    qs = [q_ref[0, 0, j * blk:(j + 1) * blk, :] for j in range(nsub)]
    own = [i * nsub + j for j in range(nsub)]
    state = [block(qs[j], own[j], jnp.zeros((blk, 1), F32), jnp.zeros((blk, HEAD_DIM), F32), True)
             for j in range(nsub)]

    def unfinished(back, laters):
        flags = [jnp.logical_and(own[j] - back >= 0, jnp.min(laters[j]) < SB_ZERO_WEIGHT_LOG)
                 for j in range(nsub)]
        return functools.reduce(jnp.logical_or, flags)

    def cond(carry):
        return carry[1]

    def body(carry):
        back, _, laters, accs = carry
        new = [block(qs[j], own[j] - back, laters[j], accs[j], False) for j in range(nsub)]
        laters, accs = [s[0] for s in new], [s[1] for s in new]
        return back + 1, unfinished(back + 1, laters), laters, accs

    state = [block(qs[j], own[j] - 1, state[j][0], state[j][1], False) for j in range(nsub)]
    laters, accs = [s[0] for s in state], [s[1] for s in state]
    _, _, _, accs = lax.while_loop(cond, body, (2, unfinished(2, laters), laters, accs))
    for j in range(nsub):
        o_ref[0, j * blk:(j + 1) * blk, :] = accs[j].astype(o_ref.dtype)


def stick_breaking_attention(qkv, blk=256, nsub=4):
    batch, _, s, dh = qkv.shape
    nh = N_HEADS_SB
    o = pl.pallas_call(
        functools.partial(_sb_kernel, blk=blk, nsub=nsub),
        grid=(batch, nh, s // (blk * nsub)),
        in_specs=[
            pl.BlockSpec((1, 1, blk * nsub, dh), lambda b, h, i: (b, SLAB_Q_SB + h, i, 0)),
            pl.BlockSpec((1, 1, s, dh), lambda b, h, i: (b, SLAB_K_SB + h, 0, 0)),
            pl.BlockSpec((1, 1, s, dh), lambda b, h, i: (b, SLAB_V_SB + h, 0, 0)),
        ],
        out_specs=pl.BlockSpec((1, blk * nsub, dh), lambda b, h, i: (b, i, h)),
        out_shape=jax.ShapeDtypeStruct((batch, s, nh * dh), BF16),
        compiler_params=_params("parallel", "parallel", "arbitrary"),
    )(qkv, qkv, qkv)
    return o.reshape(batch * s, nh * dh)


DIL_GROUP = 4


def _dil_kernel(q_ref, k_ref, v_ref, o_ref, kf_ref, vf_ref, kg_ref, vg_ref, qf_ref, qg_ref,
                po_ref, pm_ref, pl_ref, of_ref):
    i = pl.program_id(2)
    st = DIL_STEPS
    grp = DIL_GROUP
    span, dh = qf_ref.shape
    seq = kf_ref.shape[0]
    chunk = 512

    @pl.when(i == 0)
    def _():
        def conv(c, carry):
            rows = pl.ds(pl.multiple_of(c * chunk, chunk), chunk)
            kf_ref[rows, :] = k_ref[0, 0, rows, :].astype(F32)
            vf_ref[rows, :] = v_ref[0, 0, rows, :].astype(F32)
            return carry
        lax.fori_loop(0, seq // chunk, conv, 0)

        def regroup(c, carry):
            for r in range(grp):
                src = pl.ds(c * chunk * grp + r, chunk, stride=grp)
                dst = pl.ds(pl.multiple_of(r * (seq // grp) + c * chunk, chunk), chunk)
                kg_ref[dst, :] = kf_ref[src, :]
                vg_ref[dst, :] = vf_ref[src, :]
            return carry
        lax.fori_loop(0, seq // grp // chunk, regroup, 0)

    qf_ref[...] = q_ref[0, 0].astype(F32)
    for r in range(grp):
        qg_ref[r * (span // grp):(r + 1) * (span // grp), :] = (
            qf_ref[pl.ds(r, span // grp, stride=grp), :])

    diff = (lax.broadcasted_iota(jnp.int32, (st, 2 * st), 1)
            - lax.broadcasted_iota(jnp.int32, (st, 2 * st), 0))
    bias_normal = jnp.where((diff >= 0) & (diff <= st), 0.0, NEG_INF)
    bias_first = jnp.where(diff <= 0, 0.0, NEG_INF)
    ones = jnp.ones((2 * st, dh), BF16)

    def strided(start, size, stride):
        return pl.ds(start, size) if stride == 1 else pl.ds(start, size, stride=stride)

    for c, (_, dil) in enumerate(DILATED_CONFIGS):
        g = grp if dil % grp == 0 else 1
        q_buf, k_buf, v_buf = (qg_ref, kg_ref, vg_ref) if g == grp else (qf_ref, kf_ref, vf_ref)

        def unit(u, carry, c=c, dil=dil, g=g, q_buf=q_buf, k_buf=k_buf, v_buf=v_buf):
            inner = dil // g
            r, blk = u % dil, u // dil
            group_row = (r % g) * (span // g)
            q_start = group_row + blk * st * inner + r // g
            block = i * (span // g) + blk * st * inner
            first = block == 0
            w_start = (r % g) * (seq // g) + jnp.where(first, 0, block - st * inner) + r // g
            q = q_buf[strided(q_start, st, inner), :].astype(BF16)
            kw = k_buf[strided(w_start, 2 * st, inner), :].astype(BF16)
            vw = v_buf[strided(w_start, 2 * st, inner), :].astype(BF16)
            s = _dot_nt(q, kw) + jnp.where(first, bias_first, bias_normal)
            m = jnp.max(s, axis=-1, keepdims=True)
            p = jnp.exp(s - m).astype(BF16)
            o_den = _dot(p, jnp.concatenate([vw, ones], axis=1))
            rows = strided(q_start, st, inner)
            po_ref[c, rows, :] = o_den[:, :dh]
            pl_ref[c, rows, :] = o_den[:, dh:]
            pm_ref[c, rows, :] = jnp.broadcast_to(m, (st, dh))
            return carry

        lax.fori_loop(0, span // st, unit, 0, unroll=8)

    def merge(c, carry):
        rows_g = pl.ds(pl.multiple_of(c * st, st), st)
        per_group = span // grp // st
        rows_n = pl.ds((c % per_group) * st * grp + c // per_group, st, stride=grp)
        rows = [rows_g if dil % grp == 0 else rows_n for _, dil in DILATED_CONFIGS]
        ms = [pm_ref[b, rows[b], :] for b in range(len(rows))]
        m = functools.reduce(jnp.maximum, ms)
        ws = [jnp.exp(mb - m) for mb in ms]
        den = sum(w * pl_ref[b, rows[b], :] for b, w in enumerate(ws))
        num = sum(w * po_ref[b, rows[b], :] for b, w in enumerate(ws))
        of_ref[rows_n, :] = num / den
        return carry

    lax.fori_loop(0, span // st, merge, 0, unroll=2)
    o_ref[0] = of_ref[...].astype(o_ref.dtype)


def dilated_attention(qkv):
    batch, _, s, dh = qkv.shape
    nh = N_HEADS_DIL
    span = DIL_STEPS * max(dil for _, dil in DILATED_CONFIGS)
    assert s % span == 0 and s >= 2 * span
    n_cfg = len(DILATED_CONFIGS)
    o = pl.pallas_call(
        _dil_kernel,
        grid=(batch, nh, s // span),
        in_specs=[
            pl.BlockSpec((1, 1, span, dh), lambda b, h, i: (b, SLAB_Q_DIL + h, i, 0)),
            pl.BlockSpec((1, 1, s, dh), lambda b, h, i: (b, SLAB_K_DIL + h, 0, 0)),
            pl.BlockSpec((1, 1, s, dh), lambda b, h, i: (b, SLAB_V_DIL + h, 0, 0)),
        ],
        out_specs=pl.BlockSpec((1, span, dh), lambda b, h, i: (b, i, h)),
        out_shape=jax.ShapeDtypeStruct((batch, s, nh * dh), BF16),
        scratch_shapes=(
            [pltpu.VMEM((s, dh), F32)] * 4
            + [pltpu.VMEM((span, dh), F32)] * 2
            + [pltpu.VMEM((n_cfg, span, dh), F32)] * 3
            + [pltpu.VMEM((span, dh), F32)]),
        compiler_params=_params("parallel", "parallel", "arbitrary"),
    )(qkv, qkv, qkv)
    return o.reshape(batch * s, nh * dh)


def _rms(y, g):
    return y * lax.rsqrt(jnp.mean(y * y, axis=-1, keepdims=True) + EPS) * g


def _mix_out(osb_ref, odil_ref, x_ref, gsb_ref, gdil_ref, w_ref, gffn_ref):
    d_sb = osb_ref.shape[1]
    y_sb = _rms(osb_ref[...].astype(F32), gsb_ref[...]).astype(BF16)
    y_dil = _rms(odil_ref[...].astype(F32), gdil_ref[...]).astype(BF16)
    x = x_ref[...] + _dot(y_sb, w_ref[:d_sb, :]) + _dot(y_dil, w_ref[d_sb:, :])
    return x, _rms(x, gffn_ref[...])


def _outproj_router_kernel(osb_ref, odil_ref, x_ref, gsb_ref, gdil_ref, w_ref, gffn_ref, wr_ref,
                           xo_ref, xn_ref, comb_ref, cnt_ref):
    x, xn = _mix_out(osb_ref, odil_ref, x_ref, gsb_ref, gdil_ref, w_ref, gffn_ref)
    xo_ref[...] = x
    xn_ref[...] = xn.astype(BF16)

    lane = lax.broadcasted_iota(jnp.int32, comb_ref.shape, 1)
    logits = jnp.full(comb_ref.shape, -jnp.inf, F32)
    for e in range(N_EXPERTS):
        le = jnp.sum(xn * wr_ref[e:e + 1, :], axis=-1, keepdims=True)
        logits = jnp.where(lane == e, le, logits)
    top1 = jnp.max(logits, axis=-1, keepdims=True)
    idx1 = jnp.min(jnp.where(logits == top1, lane, V7X_LANES), axis=-1, keepdims=True)
    rest = jnp.where(lane == idx1, -jnp.inf, logits)
    top2 = jnp.max(rest, axis=-1, keepdims=True)
    idx2 = jnp.min(jnp.where(rest == top2, lane, V7X_LANES), axis=-1, keepdims=True)
    t = jnp.exp(top2 - top1)
    gate1 = 1.0 / (1.0 + t)
    gate2 = t / (1.0 + t)
    comb = jnp.where(lane == idx1, gate1, 0.0) + jnp.where(lane == idx2, gate2, 0.0)
    comb_ref[...] = comb
    routed = jnp.sum(jnp.where(comb > 0.0, 1.0, 0.0), axis=0, keepdims=True)
    cnt_ref[...] = jnp.broadcast_to(routed, cnt_ref.shape)


def _mix_specs(tm, d, d_sb, d_dil, index):
    row = lambda width: pl.BlockSpec((tm, width), lambda *g: (index(*g), 0))
    const = lambda shape: pl.BlockSpec(shape, lambda *g: (0, 0))
    return [row(d_sb), row(d_dil), row(d), const((1, d_sb)), const((1, d_dil)),
            const((d_sb + d_dil, d)), const((1, d))]


def out_projection_router(o_sb, o_dil, x, g_sb, g_dil, w_out, g_ffn, w_router_t, tm=512):
    n, d = x.shape
    row = lambda width: pl.BlockSpec((tm, width), lambda i: (i, 0))
    return pl.pallas_call(
        _outproj_router_kernel,
        grid=(n // tm,),
        in_specs=_mix_specs(tm, d, o_sb.shape[1], o_dil.shape[1], lambda i: i) + [
            pl.BlockSpec(w_router_t.shape, lambda i: (0, 0))],
        out_specs=[row(d), row(d), row(V7X_LANES),
                   pl.BlockSpec((V7X_SUBLANES, V7X_LANES), lambda i: (i, 0))],
        out_shape=[jax.ShapeDtypeStruct((n, d), F32), jax.ShapeDtypeStruct((n, d), BF16),
                   jax.ShapeDtypeStruct((n, V7X_LANES), F32),
                   jax.ShapeDtypeStruct((n // tm * V7X_SUBLANES, V7X_LANES), F32)],
        compiler_params=_params("parallel"),
    )(o_sb, o_dil, x, g_sb, g_dil, w_out, g_ffn, w_router_t)


def _outproj_ffn_kernel(osb_ref, odil_ref, x_ref, gsb_ref, gdil_ref, w_ref, gffn_ref,
                        wg_ref, wu_ref, wd_ref, o_ref, xn_ref):
    @pl.when(pl.program_id(1) == 0)
    def _():
        x, xn = _mix_out(osb_ref, odil_ref, x_ref, gsb_ref, gdil_ref, w_ref, gffn_ref)
        o_ref[...] = x
        xn_ref[...] = xn.astype(BF16)

    xn = xn_ref[...]
    gate = _dot(xn, wg_ref[...])
    up = _dot(xn, wu_ref[...])
    hid = gate * (1.0 / (1.0 + jnp.exp(-gate))) * up
    o_ref[...] += _dot(hid.astype(BF16), wd_ref[...])


def out_projection_ffn(o_sb, o_dil, x, g_sb, g_dil, w_out, g_ffn, wg, wu, wd, tm=512, tf=512):
    n, d = x.shape
    width = wg.shape[1]
    assert width % tf == 0
    return pl.pallas_call(
        _outproj_ffn_kernel,
        grid=(n // tm, width // tf),
        in_specs=_mix_specs(tm, d, o_sb.shape[1], o_dil.shape[1], lambda i, f: i) + [
            pl.BlockSpec((d, tf), lambda i, f: (0, f)),
            pl.BlockSpec((d, tf), lambda i, f: (0, f)),
            pl.BlockSpec((tf, d), lambda i, f: (f, 0))],
        out_specs=pl.BlockSpec((tm, d), lambda i, f: (i, 0)),
        out_shape=jax.ShapeDtypeStruct((n, d), F32),
        scratch_shapes=[pltpu.VMEM((tm, d), BF16)],
        compiler_params=_params("parallel", "arbitrary"),
    )(o_sb, o_dil, x, g_sb, g_dil, w_out, g_ffn, wg, wu, wd)


MOE_ROW_TILE = 512
MOE_SEG_BITS = (32, 16, 8, 4, 2, 1)


def _moe_layout(counts, tm, n_rows):
    n_exp = counts.shape[1]
    n16 = (counts + V7X_BF16_ROWS - 1) // V7X_BF16_ROWS
    rows = n16 * V7X_BF16_ROWS
    region = jnp.sum(rows, axis=0)
    region = (region + MOE_ROW_TILE - 1) // MOE_ROW_TILE * MOE_ROW_TILE
    ends = jnp.cumsum(region)
    tile_off = (ends - region)[None, :] + jnp.cumsum(rows, axis=0) - rows
    row_start = jnp.arange(n_rows // MOE_ROW_TILE, dtype=jnp.int32) * MOE_ROW_TILE
    tile_expert = jnp.minimum(jnp.sum(row_start[:, None] >= ends[None, :], axis=1), n_exp - 1)
    n_used = ends[-1:] // MOE_ROW_TILE
    i32 = lambda a: a.astype(jnp.int32)
    return i32(tile_off).reshape(-1), i32(n16).reshape(-1), i32(tile_expert), i32(n_used)


def _moe_tile_segments(n16_ref, tile):
    segs, start = [], jnp.int32(0)
    for e in range(N_EXPERTS):
        pieces = n16_ref[tile * N_EXPERTS + e]
        segs.append((start, pieces))
        start = start + pieces * V7X_BF16_ROWS
    return segs


def _moe_tile_slots(comb, segs, n_slots):
    tm = comb.shape[0]
    rows_e = V7X_BF16_ROWS
    comb_t = comb.T[:rows_e, :]
    routed = comb_t > 0.0
    before = (lax.broadcasted_iota(jnp.int32, (tm, tm), 0)
              < lax.broadcasted_iota(jnp.int32, (tm, tm), 1))
    rank = _dot(jnp.where(routed, 1.0, 0.0).astype(BF16),
                jnp.where(before, 1.0, 0.0).astype(BF16))
    sub = lax.broadcasted_iota(jnp.int32, (rows_e, tm), 0)
    seg_col = jnp.zeros((rows_e, tm), F32)
    for e, (start, _) in enumerate(segs):
        seg_col = jnp.where(sub == e, start.astype(F32), seg_col)
    slot = jnp.where(routed, rank + seg_col, -1.0)
    n_routed = jnp.sum(jnp.where(routed, 1.0, 0.0), axis=0, keepdims=True)
    slot_hi = jnp.max(slot, axis=0, keepdims=True)
    slot_lo = jnp.where(n_routed >= 2.0,
                        jnp.sum(jnp.where(routed, slot, 0.0), axis=0, keepdims=True) - slot_hi,
                        -1.0)
    gate_hi = jnp.sum(jnp.where(slot == slot_hi, comb_t, 0.0), axis=0, keepdims=True)
    gate_lo = jnp.sum(comb_t, axis=0, keepdims=True) - gate_hi
    s_idx = lax.broadcasted_iota(jnp.int32, (n_slots, tm), 0).astype(F32)
    hit_hi = s_idx == slot_hi
    hit_lo = s_idx == slot_lo
    onehot = jnp.where(hit_hi, 1.0, jnp.where(hit_lo, 1.0, 0.0)).astype(BF16)
    gate_slot = jnp.sum(jnp.where(hit_hi, gate_hi, jnp.where(hit_lo, gate_lo, 0.0)),
                        axis=1, keepdims=True)
    return onehot, gate_slot


def _segment_copies(buf_ref, hbm_ref, off_ref, n16_ref, tile, sems, to_hbm):
    slot = tile % 2
    copies = []
    for e, (seg_start, pieces) in enumerate(_moe_tile_segments(n16_ref, tile)):
        hbm_start = off_ref[tile * N_EXPERTS + e]
        for bit in MOE_SEG_BITS:
            done = (pieces // (2 * bit)) * (2 * bit) * V7X_BF16_ROWS
            rows = bit * V7X_BF16_ROWS
            local = buf_ref.at[slot, pl.ds(pl.multiple_of(seg_start + done, V7X_BF16_ROWS), rows)]
            remote = hbm_ref.at[pl.ds(pl.multiple_of(hbm_start + done, V7X_BF16_ROWS), rows)]
            src, dst = (local, remote) if to_hbm else (remote, local)
            copies.append(((pieces & bit) != 0,
                           pltpu.make_async_copy(src, dst, sems.at[slot, e])))
    return copies


def _start_all(copies):
    for pred, cp in copies:
        pl.when(pred)(cp.start)


def _wait_all(copies):
    for pred, cp in copies:
        pl.when(pred)(cp.wait)


def _moe_dispatch_kernel(off_ref, n16_ref, xn_ref, comb_ref, xs_in_ref, xs_ref, buf_ref, sems):
    del xs_in_ref
    tile = pl.program_id(0)
    slot = tile % 2
    d = xn_ref.shape[1]
    n_slots = buf_ref.shape[1]
    segs = _moe_tile_segments(n16_ref, tile)
    onehot, gate_slot = _moe_tile_slots(comb_ref[...], segs, n_slots)
    buf_ref[slot, :, :d] = _dot(onehot, xn_ref[...]).astype(BF16)
    g_hi = gate_slot.astype(BF16).astype(F32)
    g_lo = gate_slot - g_hi
    lane = lax.broadcasted_iota(jnp.int32, (n_slots, V7X_LANES), 1)
    buf_ref[slot, :, d:] = jnp.where(lane == 0, g_hi,
                                     jnp.where(lane == 1, g_lo, 0.0)).astype(BF16)
    copies = _segment_copies(buf_ref, xs_ref, off_ref, n16_ref, tile, sems, to_hbm=True)
    _start_all(copies)

    @pl.when(tile > 0)
    def _():
        _wait_all(_segment_copies(buf_ref, xs_ref, off_ref, n16_ref, tile - 1, sems, to_hbm=True))

    @pl.when(tile == pl.num_programs(0) - 1)
    def _():
        _wait_all(copies)


def _moe_expert_kernel(te_ref, used_ref, xs_ref, wg_ref, wu_ref, wd_ref, ys_ref):
    d = wg_ref.shape[1]
    live = pl.program_id(0) < used_ref[0]

    @pl.when(live)
    def _():
        xs = xs_ref[:, :d]
        weight = xs_ref[:, d:d + 1].astype(F32) + xs_ref[:, d + 1:d + 2].astype(F32)
        gate = _dot(xs, wg_ref[0])
        up = _dot(xs, wu_ref[0])
        hid = gate * (1.0 / (1.0 + jnp.exp(-gate))) * up * weight
        ys_ref[...] = _dot(hid.astype(BF16), wd_ref[0]).astype(BF16)

    @pl.when(jnp.logical_not(live))
    def _():
        ys_ref[...] = jnp.zeros_like(ys_ref)


def _moe_combine_kernel(off_ref, n16_ref, ys_ref, comb_ref, x_ref, o_ref, buf_ref, sems):
    tile = pl.program_id(0)
    n_slots = buf_ref.shape[1]
    fetch = lambda t: _segment_copies(buf_ref, ys_ref, off_ref, n16_ref, t, sems, to_hbm=False)

    @pl.when(tile == 0)
    def _():
        buf_ref[...] = jnp.zeros_like(buf_ref)
        _start_all(fetch(tile))

    @pl.when(tile + 1 < pl.num_programs(0))
    def _():
        _start_all(fetch(tile + 1))

    onehot, _ = _moe_tile_slots(comb_ref[...], _moe_tile_segments(n16_ref, tile), n_slots)
    _wait_all(fetch(tile))
    o_ref[...] = x_ref[...] + lax.dot_general(onehot, buf_ref[tile % 2], (((0,), (0,)), ((), ())),
                                              preferred_element_type=F32)


def moe_ffn(xn, x, comb, counts, wg, wu, wd, tm=512):
    n, d = x.shape
    n_exp, _, width = wg.shape
    n_tiles = n // tm
    seg_pad = n_exp * (V7X_BF16_ROWS - 1)
    n_slots = -(-(2 * tm + seg_pad) // V7X_LANES) * V7X_LANES
    n_rows = 2 * n + n_tiles * seg_pad + n_exp * (MOE_ROW_TILE - 1)
    n_rows = -(-n_rows // MOE_ROW_TILE) * MOE_ROW_TILE
    tile_off, n16, tile_expert, n_used = _moe_layout(counts, tm, n_rows)
    d_x = d + V7X_LANES
    any_spec = pl.BlockSpec(memory_space=pl.ANY)
    sems = pltpu.SemaphoreType.DMA((2, n_exp))

    xs = pl.pallas_call(
        _moe_dispatch_kernel,
        grid_spec=pltpu.PrefetchScalarGridSpec(
            num_scalar_prefetch=2, grid=(n_tiles,),
            in_specs=[pl.BlockSpec((tm, d), lambda i, *_: (i, 0)),
                      pl.BlockSpec((tm, V7X_LANES), lambda i, *_: (i, 0)), any_spec],
            out_specs=any_spec,
            scratch_shapes=[pltpu.VMEM((2, n_slots, d_x), BF16), sems]),
        out_shape=jax.ShapeDtypeStruct((n_rows, d_x), BF16),
        input_output_aliases={4: 0},
        compiler_params=_params("arbitrary"),
    )(tile_off, n16, xn, comb, jnp.zeros((n_rows, d_x), BF16))

    once = pl.Buffered(1)
    ys = pl.pallas_call(
        _moe_expert_kernel,
        grid_spec=pltpu.PrefetchScalarGridSpec(
            num_scalar_prefetch=2, grid=(n_rows // MOE_ROW_TILE,),
            in_specs=[pl.BlockSpec((MOE_ROW_TILE, d_x), lambda r, te, used: (r, 0)),
                      pl.BlockSpec((1, d, width), lambda r, te, used: (te[r], 0, 0),
                                   pipeline_mode=once),
                      pl.BlockSpec((1, d, width), lambda r, te, used: (te[r], 0, 0),
                                   pipeline_mode=once),
                      pl.BlockSpec((1, width, d), lambda r, te, used: (te[r], 0, 0),
                                   pipeline_mode=once)],
            out_specs=pl.BlockSpec((MOE_ROW_TILE, d), lambda r, te, used: (r, 0))),
        out_shape=jax.ShapeDtypeStruct((n_rows, d), BF16),
        compiler_params=_params("arbitrary"),
    )(tile_expert, n_used, xs, wg, wu, wd)

    return pl.pallas_call(
        _moe_combine_kernel,
        grid_spec=pltpu.PrefetchScalarGridSpec(
            num_scalar_prefetch=2, grid=(n_tiles,),
            in_specs=[any_spec, pl.BlockSpec((tm, V7X_LANES), lambda i, *_: (i, 0)),
                      pl.BlockSpec((tm, d), lambda i, *_: (i, 0))],
            out_specs=pl.BlockSpec((tm, d), lambda i, *_: (i, 0)),
            scratch_shapes=[pltpu.VMEM((2, n_slots, d), BF16), sems]),
        out_shape=jax.ShapeDtypeStruct((n, d), F32),
        compiler_params=_params("arbitrary"),
    )(tile_off, n16, ys, comb, x)


def _slab_gains(q_sb, k_sb, q_dil, k_dil):
    ones = jnp.ones((4, HEAD_DIM), F32)
    rep = lambda g: jnp.broadcast_to(g[None, :], (4, HEAD_DIM))
    return jnp.concatenate([rep(q_sb), rep(k_sb), ones, rep(q_dil), rep(k_dil), ones], axis=0)


def kernel(x, positions, norm_mix, w_in, q_norm_sb, k_norm_sb, q_norm_dil, k_norm_dil, out_norm_sb, out_norm_dil, w_out, norm_ffn, w_gate_dense, w_up_dense, w_down_dense, w_router, w_gate_moe, w_up_moe, w_down_moe):
    batch, seq, d = x.shape
    n = batch * seq
    depth = w_in.shape[0]
    xf = x.reshape(n, d)
    pos_b = jnp.broadcast_to(positions.reshape(n, 1), (n, HEAD_DIM))
    cos, sin = rope_tables(pos_b)
    for i in range(depth):
        gains = _slab_gains(q_norm_sb[i], k_norm_sb[i], q_norm_dil[i], k_norm_dil[i])
        qkv = in_projection(xf, norm_mix[i][None, :], w_in[i].astype(BF16), gains, cos, sin, batch)
        o_sb = stick_breaking_attention(qkv)
        o_dil = dilated_attention(qkv)
        j = i // 2
        mix = (o_sb, o_dil, xf, out_norm_sb[i][None, :], out_norm_dil[i][None, :],
               w_out[i].astype(BF16), norm_ffn[i][None, :])
        if i % 2 == 1:
            xf, xn, comb, cnt = out_projection_router(*mix, w_router[j].T)
            counts = cnt[::V7X_SUBLANES, :N_EXPERTS].astype(jnp.int32)
            xf = moe_ffn(xn, xf, comb, counts, w_gate_moe[j].astype(BF16),
                         w_up_moe[j].astype(BF16), w_down_moe[j].astype(BF16))
        else:
            xf = out_projection_ffn(*mix, w_gate_dense[j].astype(BF16),
                                    w_up_dense[j].astype(BF16), w_down_dense[j].astype(BF16))
    return xf.reshape(batch, seq, d)
```

```python
import functools
import math

import jax
import jax.numpy as jnp
from jax import lax
from jax.experimental import pallas as pl
from jax.experimental.pallas import tpu as pltpu

HEAD_DIM = 128
N_HEADS_SB = 4
N_HEADS_DIL = 4
N_SLABS = 3 * (N_HEADS_SB + N_HEADS_DIL)
SLAB_Q_SB, SLAB_K_SB, SLAB_V_SB = 0, 4, 8
SLAB_Q_DIL, SLAB_K_DIL, SLAB_V_DIL = 12, 16, 20
DILATED_CONFIGS = ((128, 1), (512, 4), (2048, 16))
DIL_STEPS = 128
N_EXPERTS = 8
ROPE_THETA = 10000.0
EPS = 1e-6
NEG_INF = -1e30
SB_ZERO_WEIGHT_LOG = 112.0

V7X_LANES = 128
V7X_SUBLANES = 8
V7X_BF16_ROWS = 16
V7X_VMEM_LIMIT_BYTES = 56 * 1024 * 1024

F32 = jnp.float32
BF16 = jnp.bfloat16


def _params(*semantics):
    return pltpu.CompilerParams(dimension_semantics=semantics,
                                vmem_limit_bytes=V7X_VMEM_LIMIT_BYTES)


def _dot(a, b):
    return jnp.dot(a, b, preferred_element_type=F32)


def _dot_nt(a, b):
    return lax.dot_general(a, b, (((1,), (1,)), ((), ())), preferred_element_type=F32)


def _rope_kernel(pos_ref, cos_ref, sin_ref):
    lane = lax.broadcasted_iota(jnp.int32, pos_ref.shape, 1)
    half = HEAD_DIM // 2
    pair = (lane % half).astype(F32)
    inv = jnp.exp(pair * (-2.0 * math.log(ROPE_THETA) / HEAD_DIM))
    ang = pos_ref[...].astype(F32) * inv
    cos_ref[...] = jnp.cos(ang)
    s = jnp.sin(ang)
    sin_ref[...] = jnp.where(lane < half, -s, s)


def rope_tables(pos_b, tm=1024):
    n = pos_b.shape[0]
    spec = pl.BlockSpec((tm, HEAD_DIM), lambda i: (i, 0))
    return pl.pallas_call(
        _rope_kernel,
        grid=(n // tm,),
        in_specs=[spec],
        out_specs=[spec, spec],
        out_shape=[jax.ShapeDtypeStruct((n, HEAD_DIM), F32)] * 2,
        compiler_params=_params("parallel"),
    )(pos_b)


def _inproj_kernel(x_ref, g_ref, w_ref, gains_ref, cos_ref, sin_ref, o_ref):
    scale = 1.0 / math.sqrt(HEAD_DIM)
    group = 4
    tm = x_ref.shape[0]
    for rows in (slice(0, tm // 2), slice(tm // 2, tm)):
        x = x_ref[rows, :]
        ms = jnp.mean(x * x, axis=-1, keepdims=True)
        xn = (x * lax.rsqrt(ms + EPS) * g_ref[...]).astype(BF16)
        cos = cos_ref[rows, :]
        sin = sin_ref[rows, :]
        for j0 in range(0, N_SLABS, group):
            h = _dot(xn, w_ref[:, j0 * HEAD_DIM:(j0 + group) * HEAD_DIM])
            for j in range(j0, j0 + group):
                y = h[:, (j - j0) * HEAD_DIM:(j - j0 + 1) * HEAD_DIM]
                is_v = SLAB_V_SB <= j < SLAB_Q_DIL or j >= SLAB_V_DIL
                if not is_v:
                    ms_h = jnp.mean(y * y, axis=-1, keepdims=True)
                    y = y * lax.rsqrt(ms_h + EPS) * gains_ref[j:j + 1, :]
                    if j >= SLAB_Q_DIL:
                        y = y * cos + pltpu.roll(y, HEAD_DIM // 2, 1) * sin
                    if j < SLAB_K_SB or SLAB_Q_DIL <= j < SLAB_K_DIL:
                        y = y * scale
                o_ref[0, j, rows, :] = y.astype(BF16)


def in_projection(x, g_norm, w_in, gains, cos, sin, batch, tm=512):
    n, d = x.shape
    s = n // batch
    d_in = w_in.shape[1]
    nt = s // tm
    return pl.pallas_call(
        _inproj_kernel,
        grid=(batch, nt),
        in_specs=[
            pl.BlockSpec((tm, d), lambda b, i: (b * nt + i, 0)),
            pl.BlockSpec((1, d), lambda b, i: (0, 0)),
            pl.BlockSpec((d, d_in), lambda b, i: (0, 0)),
            pl.BlockSpec((N_SLABS, HEAD_DIM), lambda b, i: (0, 0)),
            pl.BlockSpec((tm, HEAD_DIM), lambda b, i: (b * nt + i, 0)),
            pl.BlockSpec((tm, HEAD_DIM), lambda b, i: (b * nt + i, 0)),
        ],
        out_specs=pl.BlockSpec((1, N_SLABS, tm, HEAD_DIM), lambda b, i: (b, 0, i, 0)),
        out_shape=jax.ShapeDtypeStruct((batch, N_SLABS, s, HEAD_DIM), BF16),
        compiler_params=_params("parallel", "parallel"),
    )(x, g_norm, w_in, gains, cos, sin)


def _softplus(z):
    return jnp.maximum(z, 0.0) + jnp.log(1.0 + jnp.exp(-jnp.abs(z)))


def _sb_kernel(q_ref, k_ref, v_ref, o_ref, *, blk, nsub):
    i = pl.program_id(2)
    row = lax.broadcasted_iota(jnp.int32, (blk, blk), 0)
    col = lax.broadcasted_iota(jnp.int32, (blk, blk), 1)
    incl = (row >= col).astype(BF16)
    strict = col < row

    def block(q, c, later, acc, diagonal):
        start = pl.multiple_of(jnp.maximum(c, 0) * blk, blk)
        kb = k_ref[0, 0, pl.ds(start, blk), :]
        vb = v_ref[0, 0, pl.ds(start, blk), :]
        z = _dot_nt(q, kb)
        sp = _softplus(z)
        if diagonal:
            sp = jnp.where(strict, sp, 0.0)
        r = _dot(sp.astype(BF16), incl)
        a = jnp.exp(z - r - later)
        total = r[:, :1]
        if diagonal:
            a = jnp.where(strict, a, 0.0)
        else:
            a = jnp.where(c >= 0, a, 0.0)
            total = jnp.where(c >= 0, total, 0.0)
        return later + total, acc + _dot(a.astype(BF16), vb)

    qs = [q_ref[0, 0, j * blk:(j + 1) * blk, :] for j in range(nsub)]
    own = [i * nsub + j for j in range(nsub)]
    state = [block(qs[j], own[j], jnp.zeros((blk, 1), F32), jnp.zeros((blk, HEAD_DIM), F32), True)
             for j in range(nsub)]

    def unfinished(back, laters):
        flags = [jnp.logical_and(own[j] - back >= 0, jnp.min(laters[j]) < SB_ZERO_WEIGHT_LOG)
                 for j in range(nsub)]
        return functools.reduce(jnp.logical_or, flags)

    def cond(carry):
        return carry[1]

    def body(carry):
        back, _, laters, accs = carry
        new = [block(qs[j], own[j] - back, laters[j], accs[j], False) for j in range(nsub)]
        laters, accs = [s[0] for s in new], [s[1] for s in new]
        return back + 1, unfinished(back + 1, laters), laters, accs

    state = [block(qs[j], own[j] - 1, state[j][0], state[j][1], False) for j in range(nsub)]
    laters, accs = [s[0] for s in state], [s[1] for s in state]
    _, _, _, accs = lax.while_loop(cond, body, (2, unfinished(2, laters), laters, accs))
    for j in range(nsub):
        o_ref[0, j * blk:(j + 1) * blk, :] = accs[j].astype(o_ref.dtype)


def stick_breaking_attention(qkv, blk=256, nsub=4):
    batch, _, s, dh = qkv.shape
    nh = N_HEADS_SB
    o = pl.pallas_call(
        functools.partial(_sb_kernel, blk=blk, nsub=nsub),
        grid=(batch, nh, s // (blk * nsub)),
        in_specs=[
            pl.BlockSpec((1, 1, blk * nsub, dh), lambda b, h, i: (b, SLAB_Q_SB + h, i, 0)),
            pl.BlockSpec((1, 1, s, dh), lambda b, h, i: (b, SLAB_K_SB + h, 0, 0)),
            pl.BlockSpec((1, 1, s, dh), lambda b, h, i: (b, SLAB_V_SB + h, 0, 0)),
        ],
        out_specs=pl.BlockSpec((1, blk * nsub, dh), lambda b, h, i: (b, i, h)),
        out_shape=jax.ShapeDtypeStruct((batch, s, nh * dh), BF16),
        compiler_params=_params("parallel", "parallel", "arbitrary"),
    )(qkv, qkv, qkv)
    return o.reshape(batch * s, nh * dh)


DIL_GROUP = 4


def _dil_kernel(q_ref, k_ref, v_ref, o_ref, kf_ref, vf_ref, kg_ref, vg_ref, qf_ref, qg_ref,
                po_ref, pm_ref, pl_ref, of_ref):
    i = pl.program_id(2)
    st = DIL_STEPS
    grp = DIL_GROUP
    span, dh = qf_ref.shape
    seq = kf_ref.shape[0]
    chunk = 512

    @pl.when(i == 0)
    def _():
        def conv(c, carry):
            rows = pl.ds(pl.multiple_of(c * chunk, chunk), chunk)
            kf_ref[rows, :] = k_ref[0, 0, rows, :].astype(F32)
            vf_ref[rows, :] = v_ref[0, 0, rows, :].astype(F32)
            return carry
        lax.fori_loop(0, seq // chunk, conv, 0)

        def regroup(c, carry):
            for r in range(grp):
                src = pl.ds(c * chunk * grp + r, chunk, stride=grp)
                dst = pl.ds(pl.multiple_of(r * (seq // grp) + c * chunk, chunk), chunk)
                kg_ref[dst, :] = kf_ref[src, :]
                vg_ref[dst, :] = vf_ref[src, :]
            return carry
        lax.fori_loop(0, seq // grp // chunk, regroup, 0)

    qf_ref[...] = q_ref[0, 0].astype(F32)
    for r in range(grp):
        qg_ref[r * (span // grp):(r + 1) * (span // grp), :] = (
            qf_ref[pl.ds(r, span // grp, stride=grp), :])

    diff = (lax.broadcasted_iota(jnp.int32, (st, 2 * st), 1)
            - lax.broadcasted_iota(jnp.int32, (st, 2 * st), 0))
    bias_normal = jnp.where((diff >= 0) & (diff <= st), 0.0, NEG_INF)
    bias_first = jnp.where(diff <= 0, 0.0, NEG_INF)
    ones = jnp.ones((2 * st, dh), BF16)

    def strided(start, size, stride):
        return pl.ds(start, size) if stride == 1 else pl.ds(start, size, stride=stride)

    for c, (_, dil) in enumerate(DILATED_CONFIGS):
        g = grp if dil % grp == 0 else 1
        q_buf, k_buf, v_buf = (qg_ref, kg_ref, vg_ref) if g == grp else (qf_ref, kf_ref, vf_ref)

        def unit(u, carry, c=c, dil=dil, g=g, q_buf=q_buf, k_buf=k_buf, v_buf=v_buf):
            inner = dil // g
            r, blk = u % dil, u // dil
            group_row = (r % g) * (span // g)
            q_start = group_row + blk * st * inner + r // g
            block = i * (span // g) + blk * st * inner
            first = block == 0
            w_start = (r % g) * (seq // g) + jnp.where(first, 0, block - st * inner) + r // g
            q = q_buf[strided(q_start, st, inner), :].astype(BF16)
            kw = k_buf[strided(w_start, 2 * st, inner), :].astype(BF16)
            vw = v_buf[strided(w_start, 2 * st, inner), :].astype(BF16)
            s = _dot_nt(q, kw) + jnp.where(first, bias_first, bias_normal)
            m = jnp.max(s, axis=-1, keepdims=True)
            p = jnp.exp(s - m).astype(BF16)
            o_den = _dot(p, jnp.concatenate([vw, ones], axis=1))
            rows = strided(q_start, st, inner)
            po_ref[c, rows, :] = o_den[:, :dh]
            pl_ref[c, rows, :] = o_den[:, dh:]
            pm_ref[c, rows, :] = jnp.broadcast_to(m, (st, dh))
            return carry

        lax.fori_loop(0, span // st, unit, 0, unroll=8)

    def merge(c, carry):
        rows_g = pl.ds(pl.multiple_of(c * st, st), st)
        per_group = span // grp // st
        rows_n = pl.ds((c % per_group) * st * grp + c // per_group, st, stride=grp)
        rows = [rows_g if dil % grp == 0 else rows_n for _, dil in DILATED_CONFIGS]
        ms = [pm_ref[b, rows[b], :] for b in range(len(rows))]
        m = functools.reduce(jnp.maximum, ms)
        ws = [jnp.exp(mb - m) for mb in ms]
        den = sum(w * pl_ref[b, rows[b], :] for b, w in enumerate(ws))
        num = sum(w * po_ref[b, rows[b], :] for b, w in enumerate(ws))
        of_ref[rows_n, :] = num / den
        return carry

    lax.fori_loop(0, span // st, merge, 0, unroll=2)
    o_ref[0] = of_ref[...].astype(o_ref.dtype)


def dilated_attention(qkv):
    batch, _, s, dh = qkv.shape
    nh = N_HEADS_DIL
    span = DIL_STEPS * max(dil for _, dil in DILATED_CONFIGS)
    assert s % span == 0 and s >= 2 * span
    n_cfg = len(DILATED_CONFIGS)
    o = pl.pallas_call(
        _dil_kernel,
        grid=(batch, nh, s // span),
        in_specs=[
            pl.BlockSpec((1, 1, span, dh), lambda b, h, i: (b, SLAB_Q_DIL + h, i, 0)),
            pl.BlockSpec((1, 1, s, dh), lambda b, h, i: (b, SLAB_K_DIL + h, 0, 0)),
            pl.BlockSpec((1, 1, s, dh), lambda b, h, i: (b, SLAB_V_DIL + h, 0, 0)),
        ],
        out_specs=pl.BlockSpec((1, span, dh), lambda b, h, i: (b, i, h)),
        out_shape=jax.ShapeDtypeStruct((batch, s, nh * dh), BF16),
        scratch_shapes=(
            [pltpu.VMEM((s, dh), F32)] * 4
            + [pltpu.VMEM((span, dh), F32)] * 2
            + [pltpu.VMEM((n_cfg, span, dh), F32)] * 3
            + [pltpu.VMEM((span, dh), F32)]),
        compiler_params=_params("parallel", "parallel", "arbitrary"),
    )(qkv, qkv, qkv)
    return o.reshape(batch * s, nh * dh)


def _rms(y, g):
    return y * lax.rsqrt(jnp.mean(y * y, axis=-1, keepdims=True) + EPS) * g


def _mix_out(osb_ref, odil_ref, x_ref, gsb_ref, gdil_ref, w_ref, gffn_ref):
    d_sb = osb_ref.shape[1]
    y_sb = _rms(osb_ref[...].astype(F32), gsb_ref[...]).astype(BF16)
    y_dil = _rms(odil_ref[...].astype(F32), gdil_ref[...]).astype(BF16)
    x = x_ref[...] + _dot(y_sb, w_ref[:d_sb, :]) + _dot(y_dil, w_ref[d_sb:, :])
    return x, _rms(x, gffn_ref[...])


def _outproj_router_kernel(osb_ref, odil_ref, x_ref, gsb_ref, gdil_ref, w_ref, gffn_ref, wr_ref,
                           xo_ref, xn_ref, comb_ref, cnt_ref):
    x, xn = _mix_out(osb_ref, odil_ref, x_ref, gsb_ref, gdil_ref, w_ref, gffn_ref)
    xo_ref[...] = x
    xn_ref[...] = xn.astype(BF16)

    lane = lax.broadcasted_iota(jnp.int32, comb_ref.shape, 1)
    logits = jnp.full(comb_ref.shape, -jnp.inf, F32)
    for e in range(N_EXPERTS):
        le = jnp.sum(xn * wr_ref[e:e + 1, :], axis=-1, keepdims=True)
        logits = jnp.where(lane == e, le, logits)
    top1 = jnp.max(logits, axis=-1, keepdims=True)
    idx1 = jnp.min(jnp.where(logits == top1, lane, V7X_LANES), axis=-1, keepdims=True)
    rest = jnp.where(lane == idx1, -jnp.inf, logits)
    top2 = jnp.max(rest, axis=-1, keepdims=True)
    idx2 = jnp.min(jnp.where(rest == top2, lane, V7X_LANES), axis=-1, keepdims=True)
    t = jnp.exp(top2 - top1)
    gate1 = 1.0 / (1.0 + t)
    gate2 = t / (1.0 + t)
    comb = jnp.where(lane == idx1, gate1, 0.0) + jnp.where(lane == idx2, gate2, 0.0)
    comb_ref[...] = comb
    routed = jnp.sum(jnp.where(comb > 0.0, 1.0, 0.0), axis=0, keepdims=True)
    cnt_ref[...] = jnp.broadcast_to(routed, cnt_ref.shape)


def _mix_specs(tm, d, d_sb, d_dil, index):
    row = lambda width: pl.BlockSpec((tm, width), lambda *g: (index(*g), 0))
    const = lambda shape: pl.BlockSpec(shape, lambda *g: (0, 0))
    return [row(d_sb), row(d_dil), row(d), const((1, d_sb)), const((1, d_dil)),
            const((d_sb + d_dil, d)), const((1, d))]


def out_projection_router(o_sb, o_dil, x, g_sb, g_dil, w_out, g_ffn, w_router_t, tm=512):
    n, d = x.shape
    row = lambda width: pl.BlockSpec((tm, width), lambda i: (i, 0))
    return pl.pallas_call(
        _outproj_router_kernel,
        grid=(n // tm,),
        in_specs=_mix_specs(tm, d, o_sb.shape[1], o_dil.shape[1], lambda i: i) + [
            pl.BlockSpec(w_router_t.shape, lambda i: (0, 0))],
        out_specs=[row(d), row(d), row(V7X_LANES),
                   pl.BlockSpec((V7X_SUBLANES, V7X_LANES), lambda i: (i, 0))],
        out_shape=[jax.ShapeDtypeStruct((n, d), F32), jax.ShapeDtypeStruct((n, d), BF16),
                   jax.ShapeDtypeStruct((n, V7X_LANES), F32),
                   jax.ShapeDtypeStruct((n // tm * V7X_SUBLANES, V7X_LANES), F32)],
        compiler_params=_params("parallel"),
    )(o_sb, o_dil, x, g_sb, g_dil, w_out, g_ffn, w_router_t)


def _outproj_ffn_kernel(osb_ref, odil_ref, x_ref, gsb_ref, gdil_ref, w_ref, gffn_ref,
                        wg_ref, wu_ref, wd_ref, o_ref, xn_ref):
    @pl.when(pl.program_id(1) == 0)
    def _():
        x, xn = _mix_out(osb_ref, odil_ref, x_ref, gsb_ref, gdil_ref, w_ref, gffn_ref)
        o_ref[...] = x
        xn_ref[...] = xn.astype(BF16)

    xn = xn_ref[...]
    gate = _dot(xn, wg_ref[...])
    up = _dot(xn, wu_ref[...])
    hid = gate * (1.0 / (1.0 + jnp.exp(-gate))) * up
    o_ref[...] += _dot(hid.astype(BF16), wd_ref[...])


def out_projection_ffn(o_sb, o_dil, x, g_sb, g_dil, w_out, g_ffn, wg, wu, wd, tm=512, tf=512):
    n, d = x.shape
    width = wg.shape[1]
    assert width % tf == 0
    return pl.pallas_call(
        _outproj_ffn_kernel,
        grid=(n // tm, width // tf),
        in_specs=_mix_specs(tm, d, o_sb.shape[1], o_dil.shape[1], lambda i, f: i) + [
            pl.BlockSpec((d, tf), lambda i, f: (0, f)),
            pl.BlockSpec((d, tf), lambda i, f: (0, f)),
            pl.BlockSpec((tf, d), lambda i, f: (f, 0))],
        out_specs=pl.BlockSpec((tm, d), lambda i, f: (i, 0)),
        out_shape=jax.ShapeDtypeStruct((n, d), F32),
        scratch_shapes=[pltpu.VMEM((tm, d), BF16)],
        compiler_params=_params("parallel", "arbitrary"),
    )(o_sb, o_dil, x, g_sb, g_dil, w_out, g_ffn, wg, wu, wd)


MOE_ROW_TILE = 512
MOE_SEG_BITS = (32, 16, 8, 4, 2, 1)


def _moe_layout(counts, tm, n_rows):
    n_exp = counts.shape[1]
    n16 = (counts + V7X_BF16_ROWS - 1) // V7X_BF16_ROWS
    rows = n16 * V7X_BF16_ROWS
    filled = jnp.sum(rows, axis=0)
    region = (filled + MOE_ROW_TILE - 1) // MOE_ROW_TILE * MOE_ROW_TILE
    ends = jnp.cumsum(region)
    tile_off = (ends - region)[None, :] + jnp.cumsum(rows, axis=0) - rows
    tail_start = ends - region + filled
    tail_n16 = (region - filled) // V7X_BF16_ROWS
    row_start = jnp.arange(n_rows // MOE_ROW_TILE, dtype=jnp.int32) * MOE_ROW_TILE
    tile_expert = jnp.minimum(jnp.sum(row_start[:, None] >= ends[None, :], axis=1), n_exp - 1)
    n_used = ends[-1:] // MOE_ROW_TILE
    i32 = lambda a: a.astype(jnp.int32)
    return (i32(tile_off).reshape(-1), i32(n16).reshape(-1), i32(tail_start), i32(tail_n16),
            i32(tile_expert), i32(n_used))


def _moe_tile_segments(n16_ref, tile):
    segs, start = [], jnp.int32(0)
    for e in range(N_EXPERTS):
        pieces = n16_ref[tile * N_EXPERTS + e]
        segs.append((start, pieces))
        start = start + pieces * V7X_BF16_ROWS
    return segs


def _moe_tile_slots(comb, segs, n_slots):
    tm = comb.shape[0]
    rows_e = V7X_BF16_ROWS
    comb_t = comb.T[:rows_e, :]
    routed = comb_t > 0.0
    before = (lax.broadcasted_iota(jnp.int32, (tm, tm), 0)
              < lax.broadcasted_iota(jnp.int32, (tm, tm), 1))
    rank = _dot(jnp.where(routed, 1.0, 0.0).astype(BF16),
                jnp.where(before, 1.0, 0.0).astype(BF16))
    sub = lax.broadcasted_iota(jnp.int32, (rows_e, tm), 0)
    seg_col = jnp.zeros((rows_e, tm), F32)
    for e, (start, _) in enumerate(segs):
        seg_col = jnp.where(sub == e, start.astype(F32), seg_col)
    slot = jnp.where(routed, rank + seg_col, -1.0)
    n_routed = jnp.sum(jnp.where(routed, 1.0, 0.0), axis=0, keepdims=True)
    slot_hi = jnp.max(slot, axis=0, keepdims=True)
    slot_lo = jnp.where(n_routed >= 2.0,
                        jnp.sum(jnp.where(routed, slot, 0.0), axis=0, keepdims=True) - slot_hi,
                        -1.0)
    gate_hi = jnp.sum(jnp.where(slot == slot_hi, comb_t, 0.0), axis=0, keepdims=True)
    gate_lo = jnp.sum(comb_t, axis=0, keepdims=True) - gate_hi
    s_idx = lax.broadcasted_iota(jnp.int32, (n_slots, tm), 0).astype(F32)
    hit_hi = s_idx == slot_hi
    hit_lo = s_idx == slot_lo
    onehot = jnp.where(hit_hi, 1.0, jnp.where(hit_lo, 1.0, 0.0)).astype(BF16)
    gate_slot = jnp.sum(jnp.where(hit_hi, gate_hi, jnp.where(hit_lo, gate_lo, 0.0)),
                        axis=1, keepdims=True)
    return onehot, gate_slot


def _piece_copies(pieces, local_rows, hbm_ref, hbm_start, sem, to_hbm):
    copies = []
    for bit in MOE_SEG_BITS:
        done = (pieces // (2 * bit)) * (2 * bit) * V7X_BF16_ROWS
        rows = bit * V7X_BF16_ROWS
        local = local_rows(done, rows)
        remote = hbm_ref.at[pl.ds(pl.multiple_of(hbm_start + done, V7X_BF16_ROWS), rows)]
        src, dst = (local, remote) if to_hbm else (remote, local)
        copies.append(((pieces & bit) != 0, pltpu.make_async_copy(src, dst, sem)))
    return copies


def _segment_copies(buf_ref, hbm_ref, off_ref, n16_ref, tile, sems, to_hbm):
    slot = tile % 2
    copies = []
    for e, (seg_start, pieces) in enumerate(_moe_tile_segments(n16_ref, tile)):
        local_rows = lambda done, rows, seg_start=seg_start: buf_ref.at[
            slot, pl.ds(pl.multiple_of(seg_start + done, V7X_BF16_ROWS), rows)]
        copies += _piece_copies(pieces, local_rows, hbm_ref, off_ref[tile * N_EXPERTS + e],
                                sems.at[slot, e], to_hbm)
    return copies


def _start_all(copies):
    for pred, cp in copies:
        pl.when(pred)(cp.start)


def _wait_all(copies):
    for pred, cp in copies:
        pl.when(pred)(cp.wait)


def _moe_dispatch_kernel(off_ref, n16_ref, tail_start_ref, tail_n16_ref, used_ref, xn_ref,
                         comb_ref, xs_ref, buf_ref, zero_ref, sems, tail_sems):
    tile = pl.program_id(0)
    slot = tile % 2
    d = xn_ref.shape[1]
    n_slots = buf_ref.shape[1]
    segs = _moe_tile_segments(n16_ref, tile)
    onehot, gate_slot = _moe_tile_slots(comb_ref[...], segs, n_slots)
    buf_ref[slot, :, :d] = _dot(onehot, xn_ref[...]).astype(BF16)
    g_hi = gate_slot.astype(BF16).astype(F32)
    g_lo = gate_slot - g_hi
    lane = lax.broadcasted_iota(jnp.int32, (n_slots, V7X_LANES), 1)
    buf_ref[slot, :, d:] = jnp.where(lane == 0, g_hi,
                                     jnp.where(lane == 1, g_lo, 0.0)).astype(BF16)
    copies = _segment_copies(buf_ref, xs_ref, off_ref, n16_ref, tile, sems, to_hbm=True)
    _start_all(copies)

    @pl.when(tile > 0)
    def _():
        _wait_all(_segment_copies(buf_ref, xs_ref, off_ref, n16_ref, tile - 1, sems, to_hbm=True))

    @pl.when(tile == pl.num_programs(0) - 1)
    def _():
        _wait_all(copies)
        zero_ref[...] = jnp.zeros_like(zero_ref)
        tails = []
        for e in range(N_EXPERTS):
            tails += _piece_copies(tail_n16_ref[e], lambda done, rows: zero_ref.at[pl.ds(0, rows)],
                                   xs_ref, tail_start_ref[e], tail_sems.at[e], to_hbm=True)
        _start_all(tails)
        _wait_all(tails)

        def zero_row_tile(r, carry):
            rows = pl.ds(pl.multiple_of(r * MOE_ROW_TILE, MOE_ROW_TILE), MOE_ROW_TILE)
            cp = pltpu.make_async_copy(zero_ref, xs_ref.at[rows], tail_sems.at[0])
            cp.start()
            cp.wait()
            return carry
        lax.fori_loop(used_ref[0], xs_ref.shape[0] // MOE_ROW_TILE, zero_row_tile, 0)


def _moe_expert_kernel(te_ref, used_ref, xs_ref, wg_ref, wu_ref, wd_ref, ys_ref):
    d = wg_ref.shape[1]
    live = pl.program_id(0) < used_ref[0]

    @pl.when(live)
    def _():
        xs = xs_ref[:, :d]
        weight = xs_ref[:, d:d + 1].astype(F32) + xs_ref[:, d + 1:d + 2].astype(F32)
        gate = _dot(xs, wg_ref[0])
        up = _dot(xs, wu_ref[0])
        hid = gate * (1.0 / (1.0 + jnp.exp(-gate))) * up * weight
        ys_ref[...] = _dot(hid.astype(BF16), wd_ref[0]).astype(BF16)

    @pl.when(jnp.logical_not(live))
    def _():
        ys_ref[...] = jnp.zeros_like(ys_ref)


def _moe_combine_kernel(off_ref, n16_ref, ys_ref, comb_ref, x_ref, o_ref, buf_ref, sems):
    tile = pl.program_id(0)
    n_slots = buf_ref.shape[1]
    fetch = lambda t: _segment_copies(buf_ref, ys_ref, off_ref, n16_ref, t, sems, to_hbm=False)

    @pl.when(tile == 0)
    def _():
        buf_ref[...] = jnp.zeros_like(buf_ref)
        _start_all(fetch(tile))

    @pl.when(tile + 1 < pl.num_programs(0))
    def _():
        _start_all(fetch(tile + 1))

    onehot, _ = _moe_tile_slots(comb_ref[...], _moe_tile_segments(n16_ref, tile), n_slots)
    _wait_all(fetch(tile))
    o_ref[...] = x_ref[...] + lax.dot_general(onehot, buf_ref[tile % 2], (((0,), (0,)), ((), ())),
                                              preferred_element_type=F32)


def moe_ffn(xn, x, comb, counts, wg, wu, wd, tm=512):
    n, d = x.shape
    n_exp, _, width = wg.shape
    n_tiles = n // tm
    seg_pad = n_exp * (V7X_BF16_ROWS - 1)
    n_slots = -(-(2 * tm + seg_pad) // V7X_LANES) * V7X_LANES
    n_rows = 2 * n + n_tiles * seg_pad + n_exp * (MOE_ROW_TILE - 1)
    n_rows = -(-n_rows // MOE_ROW_TILE) * MOE_ROW_TILE
    tile_off, n16, tail_start, tail_n16, tile_expert, n_used = _moe_layout(counts, tm, n_rows)
    d_x = d + V7X_LANES
    any_spec = pl.BlockSpec(memory_space=pl.ANY)
    sems = pltpu.SemaphoreType.DMA((2, n_exp))

    xs = pl.pallas_call(
        _moe_dispatch_kernel,
        grid_spec=pltpu.PrefetchScalarGridSpec(
            num_scalar_prefetch=5, grid=(n_tiles,),
            in_specs=[pl.BlockSpec((tm, d), lambda i, *_: (i, 0)),
                      pl.BlockSpec((tm, V7X_LANES), lambda i, *_: (i, 0))],
            out_specs=any_spec,
            scratch_shapes=[pltpu.VMEM((2, n_slots, d_x), BF16),
                            pltpu.VMEM((MOE_ROW_TILE, d_x), BF16), sems,
                            pltpu.SemaphoreType.DMA((n_exp,))]),
        out_shape=jax.ShapeDtypeStruct((n_rows, d_x), BF16),
        compiler_params=_params("arbitrary"),
    )(tile_off, n16, tail_start, tail_n16, n_used, xn, comb)

    once = pl.Buffered(1)
    ys = pl.pallas_call(
        _moe_expert_kernel,
        grid_spec=pltpu.PrefetchScalarGridSpec(
            num_scalar_prefetch=2, grid=(n_rows // MOE_ROW_TILE,),
            in_specs=[pl.BlockSpec((MOE_ROW_TILE, d_x), lambda r, te, used: (r, 0)),
                      pl.BlockSpec((1, d, width), lambda r, te, used: (te[r], 0, 0),
                                   pipeline_mode=once),
                      pl.BlockSpec((1, d, width), lambda r, te, used: (te[r], 0, 0),
                                   pipeline_mode=once),
                      pl.BlockSpec((1, width, d), lambda r, te, used: (te[r], 0, 0),
                                   pipeline_mode=once)],
            out_specs=pl.BlockSpec((MOE_ROW_TILE, d), lambda r, te, used: (r, 0))),
        out_shape=jax.ShapeDtypeStruct((n_rows, d), BF16),
        compiler_params=_params("arbitrary"),
    )(tile_expert, n_used, xs, wg, wu, wd)

    return pl.pallas_call(
        _moe_combine_kernel,
        grid_spec=pltpu.PrefetchScalarGridSpec(
            num_scalar_prefetch=2, grid=(n_tiles,),
            in_specs=[any_spec, pl.BlockSpec((tm, V7X_LANES), lambda i, *_: (i, 0)),
                      pl.BlockSpec((tm, d), lambda i, *_: (i, 0))],
            out_specs=pl.BlockSpec((tm, d), lambda i, *_: (i, 0)),
            scratch_shapes=[pltpu.VMEM((2, n_slots, d), BF16), sems]),
        out_shape=jax.ShapeDtypeStruct((n, d), F32),
        compiler_params=_params("arbitrary"),
    )(tile_off, n16, ys, comb, x)


def _slab_gains(q_sb, k_sb, q_dil, k_dil):
    ones = jnp.ones((4, HEAD_DIM), F32)
    rep = lambda g: jnp.broadcast_to(g[None, :], (4, HEAD_DIM))
    return jnp.concatenate([rep(q_sb), rep(k_sb), ones, rep(q_dil), rep(k_dil), ones], axis=0)


def kernel(x, positions, norm_mix, w_in, q_norm_sb, k_norm_sb, q_norm_dil, k_norm_dil, out_norm_sb, out_norm_dil, w_out, norm_ffn, w_gate_dense, w_up_dense, w_down_dense, w_router, w_gate_moe, w_up_moe, w_down_moe):
    batch, seq, d = x.shape
    n = batch * seq
    depth = w_in.shape[0]
    xf = x.reshape(n, d)
    pos_b = jnp.broadcast_to(positions.reshape(n, 1), (n, HEAD_DIM))
    cos, sin = rope_tables(pos_b)
    for i in range(depth):
        gains = _slab_gains(q_norm_sb[i], k_norm_sb[i], q_norm_dil[i], k_norm_dil[i])
        qkv = in_projection(xf, norm_mix[i][None, :], w_in[i].astype(BF16), gains, cos, sin, batch)
        o_sb = stick_breaking_attention(qkv)
        o_dil = dilated_attention(qkv)
        j = i // 2
        mix = (o_sb, o_dil, xf, out_norm_sb[i][None, :], out_norm_dil[i][None, :],
               w_out[i].astype(BF16), norm_ffn[i][None, :])
        if i % 2 == 1:
            xf, xn, comb, cnt = out_projection_router(*mix, w_router[j].T)
            counts = cnt[::V7X_SUBLANES, :N_EXPERTS].astype(jnp.int32)
            xf = moe_ffn(xn, xf, comb, counts, w_gate_moe[j].astype(BF16),
                         w_up_moe[j].astype(BF16), w_down_moe[j].astype(BF16))
        else:
            xf = out_projection_ffn(*mix, w_gate_dense[j].astype(BF16),
                                    w_up_dense[j].astype(BF16), w_down_dense[j].astype(BF16))
    return xf.reshape(batch, seq, d)
```

```python
import functools
import math

import jax
import jax.numpy as jnp
from jax import lax
from jax.experimental import pallas as pl
from jax.experimental.pallas import tpu as pltpu

HEAD_DIM = 128
N_HEADS_SB = 4
N_HEADS_DIL = 4
N_SLABS = 3 * (N_HEADS_SB + N_HEADS_DIL)
SLAB_Q_SB, SLAB_K_SB, SLAB_V_SB = 0, 4, 8
SLAB_Q_DIL, SLAB_K_DIL, SLAB_V_DIL = 12, 16, 20
DILATED_CONFIGS = ((128, 1), (512, 4), (2048, 16))
DIL_STEPS = 128
N_EXPERTS = 8
ROPE_THETA = 10000.0
EPS = 1e-6
NEG_INF = -1e30
SB_ZERO_WEIGHT_LOG = 112.0

V7X_LANES = 128
V7X_SUBLANES = 8
V7X_BF16_ROWS = 16
V7X_VMEM_LIMIT_BYTES = 56 * 1024 * 1024

F32 = jnp.float32
BF16 = jnp.bfloat16


def _params(*semantics):
    return pltpu.CompilerParams(dimension_semantics=semantics,
                                vmem_limit_bytes=V7X_VMEM_LIMIT_BYTES)


def _dot(a, b):
    return jnp.dot(a, b, preferred_element_type=F32)


def _dot_nt(a, b):
    return lax.dot_general(a, b, (((1,), (1,)), ((), ())), preferred_element_type=F32)


def _rope_kernel(pos_ref, cos_ref, sin_ref):
    lane = lax.broadcasted_iota(jnp.int32, pos_ref.shape, 1)
    half = HEAD_DIM // 2
    pair = (lane % half).astype(F32)
    inv = jnp.exp(pair * (-2.0 * math.log(ROPE_THETA) / HEAD_DIM))
    ang = pos_ref[...].astype(F32) * inv
    cos_ref[...] = jnp.cos(ang)
    s = jnp.sin(ang)
    sin_ref[...] = jnp.where(lane < half, -s, s)


def rope_tables(pos_b, tm=1024):
    n = pos_b.shape[0]
    spec = pl.BlockSpec((tm, HEAD_DIM), lambda i: (i, 0))
    return pl.pallas_call(
        _rope_kernel,
        grid=(n // tm,),
        in_specs=[spec],
        out_specs=[spec, spec],
        out_shape=[jax.ShapeDtypeStruct((n, HEAD_DIM), F32)] * 2,
        compiler_params=_params("parallel"),
    )(pos_b)


def _inproj_kernel(x_ref, g_ref, w_ref, gains_ref, cos_ref, sin_ref, o_ref):
    scale = 1.0 / math.sqrt(HEAD_DIM)
    group = 4
    tm = x_ref.shape[0]
    for rows in (slice(0, tm // 2), slice(tm // 2, tm)):
        x = x_ref[rows, :]
        ms = jnp.mean(x * x, axis=-1, keepdims=True)
        xn = (x * lax.rsqrt(ms + EPS) * g_ref[...]).astype(BF16)
        cos = cos_ref[rows, :]
        sin = sin_ref[rows, :]
        for j0 in range(0, N_SLABS, group):
            h = _dot(xn, w_ref[:, j0 * HEAD_DIM:(j0 + group) * HEAD_DIM])
            for j in range(j0, j0 + group):
                y = h[:, (j - j0) * HEAD_DIM:(j - j0 + 1) * HEAD_DIM]
                is_v = SLAB_V_SB <= j < SLAB_Q_DIL or j >= SLAB_V_DIL
                if not is_v:
                    ms_h = jnp.mean(y * y, axis=-1, keepdims=True)
                    y = y * lax.rsqrt(ms_h + EPS) * gains_ref[j:j + 1, :]
                    if j >= SLAB_Q_DIL:
                        y = y * cos + pltpu.roll(y, HEAD_DIM // 2, 1) * sin
                    if j < SLAB_K_SB or SLAB_Q_DIL <= j < SLAB_K_DIL:
                        y = y * scale
                o_ref[0, j, rows, :] = y.astype(BF16)


def in_projection(x, g_norm, w_in, gains, cos, sin, batch, tm=512):
    n, d = x.shape
    s = n // batch
    d_in = w_in.shape[1]
    nt = s // tm
    return pl.pallas_call(
        _inproj_kernel,
        grid=(batch, nt),
        in_specs=[
            pl.BlockSpec((tm, d), lambda b, i: (b * nt + i, 0)),
            pl.BlockSpec((1, d), lambda b, i: (0, 0)),
            pl.BlockSpec((d, d_in), lambda b, i: (0, 0)),
            pl.BlockSpec((N_SLABS, HEAD_DIM), lambda b, i: (0, 0)),
            pl.BlockSpec((tm, HEAD_DIM), lambda b, i: (b * nt + i, 0)),
            pl.BlockSpec((tm, HEAD_DIM), lambda b, i: (b * nt + i, 0)),
        ],
        out_specs=pl.BlockSpec((1, N_SLABS, tm, HEAD_DIM), lambda b, i: (b, 0, i, 0)),
        out_shape=jax.ShapeDtypeStruct((batch, N_SLABS, s, HEAD_DIM), BF16),
        compiler_params=_params("parallel", "parallel"),
    )(x, g_norm, w_in, gains, cos, sin)


def _softplus(z):
    return jnp.maximum(z, 0.0) + jnp.log(1.0 + jnp.exp(-jnp.abs(z)))


def _sb_kernel(q_ref, k_ref, v_ref, o_ref, *, blk, nsub):
    i = pl.program_id(2)
    row = lax.broadcasted_iota(jnp.int32, (blk, blk), 0)
    col = lax.broadcasted_iota(jnp.int32, (blk, blk), 1)
    incl = (row >= col).astype(BF16)
    strict = col < row

    def block(q, c, later, acc, diagonal):
        start = pl.multiple_of(jnp.maximum(c, 0) * blk, blk)
        kb = k_ref[0, 0, pl.ds(start, blk), :]
        vb = v_ref[0, 0, pl.ds(start, blk), :]
        z = _dot_nt(q, kb)
        sp = _softplus(z)
        if diagonal:
            sp = jnp.where(strict, sp, 0.0)
        r = _dot(sp.astype(BF16), incl)
        a = jnp.exp(z - r - later)
        total = r[:, :1]
        if diagonal:
            a = jnp.where(strict, a, 0.0)
        else:
            a = jnp.where(c >= 0, a, 0.0)
            total = jnp.where(c >= 0, total, 0.0)
        return later + total, acc + _dot(a.astype(BF16), vb)

    qs = [q_ref[0, 0, j * blk:(j + 1) * blk, :] for j in range(nsub)]
    own = [i * nsub + j for j in range(nsub)]
    state = [block(qs[j], own[j], jnp.zeros((blk, 1), F32), jnp.zeros((blk, HEAD_DIM), F32), True)
             for j in range(nsub)]

    def unfinished(back, laters):
        flags = [jnp.logical_and(own[j] - back >= 0, jnp.min(laters[j]) < SB_ZERO_WEIGHT_LOG)
                 for j in range(nsub)]
        return functools.reduce(jnp.logical_or, flags)

    def cond(carry):
        return carry[1]

    def body(carry):
        back, _, laters, accs = carry
        new = [block(qs[j], own[j] - back, laters[j], accs[j], False) for j in range(nsub)]
        laters, accs = [s[0] for s in new], [s[1] for s in new]
        return back + 1, unfinished(back + 1, laters), laters, accs

    state = [block(qs[j], own[j] - 1, state[j][0], state[j][1], False) for j in range(nsub)]
    laters, accs = [s[0] for s in state], [s[1] for s in state]
    _, _, _, accs = lax.while_loop(cond, body, (2, unfinished(2, laters), laters, accs))
    for j in range(nsub):
        o_ref[0, j * blk:(j + 1) * blk, :] = accs[j].astype(o_ref.dtype)


def stick_breaking_attention(qkv, blk=256, nsub=4):
    batch, _, s, dh = qkv.shape
    nh = N_HEADS_SB
    o = pl.pallas_call(
        functools.partial(_sb_kernel, blk=blk, nsub=nsub),
        grid=(batch, nh, s // (blk * nsub)),
        in_specs=[
            pl.BlockSpec((1, 1, blk * nsub, dh), lambda b, h, i: (b, SLAB_Q_SB + h, i, 0)),
            pl.BlockSpec((1, 1, s, dh), lambda b, h, i: (b, SLAB_K_SB + h, 0, 0)),
            pl.BlockSpec((1, 1, s, dh), lambda b, h, i: (b, SLAB_V_SB + h, 0, 0)),
        ],
        out_specs=pl.BlockSpec((1, blk * nsub, dh), lambda b, h, i: (b, i, h)),
        out_shape=jax.ShapeDtypeStruct((batch, s, nh * dh), BF16),
        compiler_params=_params("parallel", "parallel", "arbitrary"),
    )(qkv, qkv, qkv)
    return o.reshape(batch * s, nh * dh)


DIL_GROUP = 4


def _dil_kernel(q_ref, k_ref, v_ref, o_ref, kf_ref, vf_ref, kg_ref, vg_ref, qf_ref, qg_ref,
                po_ref, pm_ref, pl_ref, of_ref):
    i = pl.program_id(2)
    st = DIL_STEPS
    grp = DIL_GROUP
    span, dh = qf_ref.shape
    seq = kf_ref.shape[0]
    chunk = 512

    @pl.when(i == 0)
    def _():
        def conv(c, carry):
            rows = pl.ds(pl.multiple_of(c * chunk, chunk), chunk)
            kf_ref[rows, :] = k_ref[0, 0, rows, :].astype(F32)
            vf_ref[rows, :] = v_ref[0, 0, rows, :].astype(F32)
            return carry
        lax.fori_loop(0, seq // chunk, conv, 0)

        def regroup(c, carry):
            for r in range(grp):
                src = pl.ds(c * chunk * grp + r, chunk, stride=grp)
                dst = pl.ds(pl.multiple_of(r * (seq // grp) + c * chunk, chunk), chunk)
                kg_ref[dst, :] = kf_ref[src, :]
                vg_ref[dst, :] = vf_ref[src, :]
            return carry
        lax.fori_loop(0, seq // grp // chunk, regroup, 0)

    qf_ref[...] = q_ref[0, 0].astype(F32)
    for r in range(grp):
        qg_ref[r * (span // grp):(r + 1) * (span // grp), :] = (
            qf_ref[pl.ds(r, span // grp, stride=grp), :])

    diff = (lax.broadcasted_iota(jnp.int32, (st, 2 * st), 1)
            - lax.broadcasted_iota(jnp.int32, (st, 2 * st), 0))
    bias_normal = jnp.where((diff >= 0) & (diff <= st), 0.0, NEG_INF)
    bias_first = jnp.where(diff <= 0, 0.0, NEG_INF)
    ones = jnp.ones((2 * st, dh), BF16)

    def strided(start, size, stride):
        return pl.ds(start, size) if stride == 1 else pl.ds(start, size, stride=stride)

    for c, (_, dil) in enumerate(DILATED_CONFIGS):
        g = grp if dil % grp == 0 else 1
        q_buf, k_buf, v_buf = (qg_ref, kg_ref, vg_ref) if g == grp else (qf_ref, kf_ref, vf_ref)

        def unit(u, carry, c=c, dil=dil, g=g, q_buf=q_buf, k_buf=k_buf, v_buf=v_buf):
            inner = dil // g
            r, blk = u % dil, u // dil
            group_row = (r % g) * (span // g)
            q_start = group_row + blk * st * inner + r // g
            block = i * (span // g) + blk * st * inner
            first = block == 0
            w_start = (r % g) * (seq // g) + jnp.where(first, 0, block - st * inner) + r // g
            q = q_buf[strided(q_start, st, inner), :].astype(BF16)
            kw = k_buf[strided(w_start, 2 * st, inner), :].astype(BF16)
            vw = v_buf[strided(w_start, 2 * st, inner), :].astype(BF16)
            s = _dot_nt(q, kw) + jnp.where(first, bias_first, bias_normal)
            m = jnp.max(s, axis=-1, keepdims=True)
            p = jnp.exp(s - m).astype(BF16)
            o_den = _dot(p, jnp.concatenate([vw, ones], axis=1))
            rows = strided(q_start, st, inner)
            po_ref[c, rows, :] = o_den[:, :dh]
            pl_ref[c, rows, :] = o_den[:, dh:]
            pm_ref[c, rows, :] = jnp.broadcast_to(m, (st, dh))
            return carry

        lax.fori_loop(0, span // st, unit, 0, unroll=8)

    def merge(c, carry):
        rows_g = pl.ds(pl.multiple_of(c * st, st), st)
        per_group = span // grp // st
        rows_n = pl.ds((c % per_group) * st * grp + c // per_group, st, stride=grp)
        rows = [rows_g if dil % grp == 0 else rows_n for _, dil in DILATED_CONFIGS]
        ms = [pm_ref[b, rows[b], :] for b in range(len(rows))]
        m = functools.reduce(jnp.maximum, ms)
        ws = [jnp.exp(mb - m) for mb in ms]
        den = sum(w * pl_ref[b, rows[b], :] for b, w in enumerate(ws))
        num = sum(w * po_ref[b, rows[b], :] for b, w in enumerate(ws))
        of_ref[rows_n, :] = num / den
        return carry

    lax.fori_loop(0, span // st, merge, 0, unroll=2)
    o_ref[0] = of_ref[...].astype(o_ref.dtype)


def dilated_attention(qkv):
    batch, _, s, dh = qkv.shape
    nh = N_HEADS_DIL
    span = DIL_STEPS * max(dil for _, dil in DILATED_CONFIGS)
    assert s % span == 0 and s >= 2 * span
    n_cfg = len(DILATED_CONFIGS)
    o = pl.pallas_call(
        _dil_kernel,
        grid=(batch, nh, s // span),
        in_specs=[
            pl.BlockSpec((1, 1, span, dh), lambda b, h, i: (b, SLAB_Q_DIL + h, i, 0)),
            pl.BlockSpec((1, 1, s, dh), lambda b, h, i: (b, SLAB_K_DIL + h, 0, 0)),
            pl.BlockSpec((1, 1, s, dh), lambda b, h, i: (b, SLAB_V_DIL + h, 0, 0)),
        ],
        out_specs=pl.BlockSpec((1, span, dh), lambda b, h, i: (b, i, h)),
        out_shape=jax.ShapeDtypeStruct((batch, s, nh * dh), BF16),
        scratch_shapes=(
            [pltpu.VMEM((s, dh), F32)] * 4
            + [pltpu.VMEM((span, dh), F32)] * 2
            + [pltpu.VMEM((n_cfg, span, dh), F32)] * 3
            + [pltpu.VMEM((span, dh), F32)]),
        compiler_params=_params("parallel", "parallel", "arbitrary"),
    )(qkv, qkv, qkv)
    return o.reshape(batch * s, nh * dh)


def _rms(y, g):
    return y * lax.rsqrt(jnp.mean(y * y, axis=-1, keepdims=True) + EPS) * g


def _mix_out(osb_ref, odil_ref, x_ref, gsb_ref, gdil_ref, w_ref, gffn_ref):
    d_sb = osb_ref.shape[1]
    y_sb = _rms(osb_ref[...].astype(F32), gsb_ref[...]).astype(BF16)
    y_dil = _rms(odil_ref[...].astype(F32), gdil_ref[...]).astype(BF16)
    x = x_ref[...] + _dot(y_sb, w_ref[:d_sb, :]) + _dot(y_dil, w_ref[d_sb:, :])
    return x, _rms(x, gffn_ref[...])


def _outproj_router_kernel(osb_ref, odil_ref, x_ref, gsb_ref, gdil_ref, w_ref, gffn_ref, wr_ref,
                           xo_ref, xn_ref, comb_ref, cnt_ref):
    tm = x_ref.shape[0]
    x, xn = _mix_out(osb_ref, odil_ref, x_ref, gsb_ref, gdil_ref, w_ref, gffn_ref)
    xo_ref[...] = x
    xn_hi = xn.astype(BF16)
    xn_ref[...] = xn_hi

    n_e = N_EXPERTS
    xn_lo = (xn - xn_hi.astype(F32)).astype(BF16)
    wr = wr_ref[...]
    wr_hi = wr.astype(BF16).astype(F32)
    lane_w = lax.broadcasted_iota(jnp.int32, wr.shape, 1)
    rhs = jnp.where(lane_w < n_e, wr_hi, wr - wr_hi).astype(BF16)
    prod = _dot(jnp.concatenate([xn_hi, xn_lo], axis=0), rhs)
    hi_rows = prod[:tm]
    logits = hi_rows + pltpu.roll(hi_rows, V7X_LANES - n_e, 1) + prod[tm:]
    lane = lax.broadcasted_iota(jnp.int32, comb_ref.shape, 1)
    logits = jnp.where(lane < n_e, logits, -jnp.inf)
    top1 = jnp.max(logits, axis=-1, keepdims=True)
    idx1 = jnp.min(jnp.where(logits == top1, lane, V7X_LANES), axis=-1, keepdims=True)
    rest = jnp.where(lane == idx1, -jnp.inf, logits)
    top2 = jnp.max(rest, axis=-1, keepdims=True)
    idx2 = jnp.min(jnp.where(rest == top2, lane, V7X_LANES), axis=-1, keepdims=True)
    t = jnp.exp(top2 - top1)
    gate1 = 1.0 / (1.0 + t)
    gate2 = t / (1.0 + t)
    comb = jnp.where(lane == idx1, gate1, 0.0) + jnp.where(lane == idx2, gate2, 0.0)
    comb_ref[...] = comb
    routed = jnp.sum(jnp.where(comb > 0.0, 1.0, 0.0), axis=0, keepdims=True)
    cnt_ref[...] = jnp.broadcast_to(routed, cnt_ref.shape)


def _mix_specs(tm, d, d_sb, d_dil, index):
    row = lambda width: pl.BlockSpec((tm, width), lambda *g: (index(*g), 0))
    const = lambda shape: pl.BlockSpec(shape, lambda *g: (0, 0))
    return [row(d_sb), row(d_dil), row(d), const((1, d_sb)), const((1, d_dil)),
            const((d_sb + d_dil, d)), const((1, d))]


def out_projection_router(o_sb, o_dil, x, g_sb, g_dil, w_out, g_ffn, w_router, tm=512):
    n, d = x.shape
    n_e = w_router.shape[1]
    w_router_2x = jnp.pad(jnp.concatenate([w_router, w_router], axis=1),
                         ((0, 0), (0, V7X_LANES - 2 * n_e)))
    row = lambda width: pl.BlockSpec((tm, width), lambda i: (i, 0))
    return pl.pallas_call(
        _outproj_router_kernel,
        grid=(n // tm,),
        in_specs=_mix_specs(tm, d, o_sb.shape[1], o_dil.shape[1], lambda i: i) + [
            pl.BlockSpec(w_router_2x.shape, lambda i: (0, 0))],
        out_specs=[row(d), row(d), row(V7X_LANES),
                   pl.BlockSpec((V7X_SUBLANES, V7X_LANES), lambda i: (i, 0))],
        out_shape=[jax.ShapeDtypeStruct((n, d), F32), jax.ShapeDtypeStruct((n, d), BF16),
                   jax.ShapeDtypeStruct((n, V7X_LANES), F32),
                   jax.ShapeDtypeStruct((n // tm * V7X_SUBLANES, V7X_LANES), F32)],
        compiler_params=_params("parallel"),
    )(o_sb, o_dil, x, g_sb, g_dil, w_out, g_ffn, w_router_2x)


def _outproj_ffn_kernel(osb_ref, odil_ref, x_ref, gsb_ref, gdil_ref, w_ref, gffn_ref,
                        wg_ref, wu_ref, wd_ref, o_ref, xn_ref):
    @pl.when(pl.program_id(1) == 0)
    def _():
        x, xn = _mix_out(osb_ref, odil_ref, x_ref, gsb_ref, gdil_ref, w_ref, gffn_ref)
        o_ref[...] = x
        xn_ref[...] = xn.astype(BF16)

    xn = xn_ref[...]
    gate = _dot(xn, wg_ref[...])
    up = _dot(xn, wu_ref[...])
    hid = gate * (1.0 / (1.0 + jnp.exp(-gate))) * up
    o_ref[...] += _dot(hid.astype(BF16), wd_ref[...])


def out_projection_ffn(o_sb, o_dil, x, g_sb, g_dil, w_out, g_ffn, wg, wu, wd, tm=512, tf=512):
    n, d = x.shape
    width = wg.shape[1]
    assert width % tf == 0
    return pl.pallas_call(
        _outproj_ffn_kernel,
        grid=(n // tm, width // tf),
        in_specs=_mix_specs(tm, d, o_sb.shape[1], o_dil.shape[1], lambda i, f: i) + [
            pl.BlockSpec((d, tf), lambda i, f: (0, f)),
            pl.BlockSpec((d, tf), lambda i, f: (0, f)),
            pl.BlockSpec((tf, d), lambda i, f: (f, 0))],
        out_specs=pl.BlockSpec((tm, d), lambda i, f: (i, 0)),
        out_shape=jax.ShapeDtypeStruct((n, d), F32),
        scratch_shapes=[pltpu.VMEM((tm, d), BF16)],
        compiler_params=_params("parallel", "arbitrary"),
    )(o_sb, o_dil, x, g_sb, g_dil, w_out, g_ffn, wg, wu, wd)


MOE_ROW_TILE = 512
MOE_SEG_BITS = (32, 16, 8, 4, 2, 1)


def _moe_layout(counts, tm, n_rows):
    n_exp = counts.shape[1]
    n16 = (counts + V7X_BF16_ROWS - 1) // V7X_BF16_ROWS
    rows = n16 * V7X_BF16_ROWS
    filled = jnp.sum(rows, axis=0)
    region = (filled + MOE_ROW_TILE - 1) // MOE_ROW_TILE * MOE_ROW_TILE
    ends = jnp.cumsum(region)
    tile_off = (ends - region)[None, :] + jnp.cumsum(rows, axis=0) - rows
    tail_start = ends - region + filled
    tail_n16 = (region - filled) // V7X_BF16_ROWS
    row_start = jnp.arange(n_rows // MOE_ROW_TILE, dtype=jnp.int32) * MOE_ROW_TILE
    tile_expert = jnp.minimum(jnp.sum(row_start[:, None] >= ends[None, :], axis=1), n_exp - 1)
    n_used = ends[-1:] // MOE_ROW_TILE
    i32 = lambda a: a.astype(jnp.int32)
    return (i32(tile_off).reshape(-1), i32(n16).reshape(-1), i32(tail_start), i32(tail_n16),
            i32(tile_expert), i32(n_used))


def _moe_tile_segments(n16_ref, tile):
    segs, start = [], jnp.int32(0)
    for e in range(N_EXPERTS):
        pieces = n16_ref[tile * N_EXPERTS + e]
        segs.append((start, pieces))
        start = start + pieces * V7X_BF16_ROWS
    return segs


def _moe_tile_slots(comb, segs, n_slots):
    tm = comb.shape[0]
    rows_e = V7X_BF16_ROWS
    comb_t = comb.T[:rows_e, :]
    routed = comb_t > 0.0
    before = (lax.broadcasted_iota(jnp.int32, (tm, tm), 0)
              < lax.broadcasted_iota(jnp.int32, (tm, tm), 1))
    rank = _dot(jnp.where(routed, 1.0, 0.0).astype(BF16),
                jnp.where(before, 1.0, 0.0).astype(BF16))
    sub = lax.broadcasted_iota(jnp.int32, (rows_e, tm), 0)
    seg_col = jnp.zeros((rows_e, tm), F32)
    for e, (start, _) in enumerate(segs):
        seg_col = jnp.where(sub == e, start.astype(F32), seg_col)
    slot = jnp.where(routed, rank + seg_col, -1.0)
    n_routed = jnp.sum(jnp.where(routed, 1.0, 0.0), axis=0, keepdims=True)
    slot_hi = jnp.max(slot, axis=0, keepdims=True)
    slot_lo = jnp.where(n_routed >= 2.0,
                        jnp.sum(jnp.where(routed, slot, 0.0), axis=0, keepdims=True) - slot_hi,
                        -1.0)
    gate_hi = jnp.sum(jnp.where(slot == slot_hi, comb_t, 0.0), axis=0, keepdims=True)
    gate_lo = jnp.sum(comb_t, axis=0, keepdims=True) - gate_hi
    s_idx = lax.broadcasted_iota(jnp.int32, (n_slots, tm), 0).astype(F32)
    hit_hi = s_idx == slot_hi
    hit_lo = s_idx == slot_lo
    onehot = jnp.where(hit_hi, 1.0, jnp.where(hit_lo, 1.0, 0.0)).astype(BF16)
    gate_slot = jnp.sum(jnp.where(hit_hi, gate_hi, jnp.where(hit_lo, gate_lo, 0.0)),
                        axis=1, keepdims=True)
    return onehot, gate_slot


def _piece_copies(pieces, local_rows, hbm_ref, hbm_start, sem, to_hbm):
    copies = []
    for bit in MOE_SEG_BITS:
        done = (pieces // (2 * bit)) * (2 * bit) * V7X_BF16_ROWS
        rows = bit * V7X_BF16_ROWS
        local = local_rows(done, rows)
        remote = hbm_ref.at[pl.ds(pl.multiple_of(hbm_start + done, V7X_BF16_ROWS), rows)]
        src, dst = (local, remote) if to_hbm else (remote, local)
        copies.append(((pieces & bit) != 0, pltpu.make_async_copy(src, dst, sem)))
    return copies


def _segment_copies(buf_ref, hbm_ref, off_ref, n16_ref, tile, sems, to_hbm):
    slot = tile % 2
    copies = []
    for e, (seg_start, pieces) in enumerate(_moe_tile_segments(n16_ref, tile)):
        local_rows = lambda done, rows, seg_start=seg_start: buf_ref.at[
            slot, pl.ds(pl.multiple_of(seg_start + done, V7X_BF16_ROWS), rows)]
        copies += _piece_copies(pieces, local_rows, hbm_ref, off_ref[tile * N_EXPERTS + e],
                                sems.at[slot, e], to_hbm)
    return copies


def _start_all(copies):
    for pred, cp in copies:
        pl.when(pred)(cp.start)


def _wait_all(copies):
    for pred, cp in copies:
        pl.when(pred)(cp.wait)


def _moe_dispatch_kernel(off_ref, n16_ref, tail_start_ref, tail_n16_ref, used_ref, xn_ref,
                         comb_ref, xs_ref, buf_ref, zero_ref, sems, tail_sems):
    tile = pl.program_id(0)
    slot = tile % 2
    d = xn_ref.shape[1]
    n_slots = buf_ref.shape[1]
    segs = _moe_tile_segments(n16_ref, tile)
    onehot, gate_slot = _moe_tile_slots(comb_ref[...], segs, n_slots)
    buf_ref[slot, :, :d] = _dot(onehot, xn_ref[...]).astype(BF16)
    g_hi = gate_slot.astype(BF16).astype(F32)
    g_lo = gate_slot - g_hi
    lane = lax.broadcasted_iota(jnp.int32, (n_slots, V7X_LANES), 1)
    buf_ref[slot, :, d:] = jnp.where(lane == 0, g_hi,
                                     jnp.where(lane == 1, g_lo, 0.0)).astype(BF16)
    copies = _segment_copies(buf_ref, xs_ref, off_ref, n16_ref, tile, sems, to_hbm=True)
    _start_all(copies)

    @pl.when(tile > 0)
    def _():
        _wait_all(_segment_copies(buf_ref, xs_ref, off_ref, n16_ref, tile - 1, sems, to_hbm=True))

    @pl.when(tile == pl.num_programs(0) - 1)
    def _():
        _wait_all(copies)
        zero_ref[...] = jnp.zeros_like(zero_ref)
        tails = []
        for e in range(N_EXPERTS):
            tails += _piece_copies(tail_n16_ref[e], lambda done, rows: zero_ref.at[pl.ds(0, rows)],
                                   xs_ref, tail_start_ref[e], tail_sems.at[e], to_hbm=True)
        _start_all(tails)
        _wait_all(tails)

        def zero_row_tile(r, carry):
            rows = pl.ds(pl.multiple_of(r * MOE_ROW_TILE, MOE_ROW_TILE), MOE_ROW_TILE)
            cp = pltpu.make_async_copy(zero_ref, xs_ref.at[rows], tail_sems.at[0])
            cp.start()
            cp.wait()
            return carry
        lax.fori_loop(used_ref[0], xs_ref.shape[0] // MOE_ROW_TILE, zero_row_tile, 0)


def _moe_expert_kernel(te_ref, used_ref, xs_ref, wg_ref, wu_ref, wd_ref, ys_ref):
    d = wg_ref.shape[1]
    live = pl.program_id(0) < used_ref[0]

    @pl.when(live)
    def _():
        xs = xs_ref[:, :d]
        weight = xs_ref[:, d:d + 1].astype(F32) + xs_ref[:, d + 1:d + 2].astype(F32)
        gate = _dot(xs, wg_ref[0])
        up = _dot(xs, wu_ref[0])
        hid = gate * (1.0 / (1.0 + jnp.exp(-gate))) * up * weight
        ys_ref[...] = _dot(hid.astype(BF16), wd_ref[0]).astype(BF16)

    @pl.when(jnp.logical_not(live))
    def _():
        ys_ref[...] = jnp.zeros_like(ys_ref)


def _moe_combine_kernel(off_ref, n16_ref, ys_ref, comb_ref, x_ref, o_ref, buf_ref, sems):
    tile = pl.program_id(0)
    n_slots = buf_ref.shape[1]
    fetch = lambda t: _segment_copies(buf_ref, ys_ref, off_ref, n16_ref, t, sems, to_hbm=False)

    @pl.when(tile == 0)
    def _():
        buf_ref[...] = jnp.zeros_like(buf_ref)
        _start_all(fetch(tile))

    @pl.when(tile + 1 < pl.num_programs(0))
    def _():
        _start_all(fetch(tile + 1))

    onehot, _ = _moe_tile_slots(comb_ref[...], _moe_tile_segments(n16_ref, tile), n_slots)
    _wait_all(fetch(tile))
    o_ref[...] = x_ref[...] + lax.dot_general(onehot, buf_ref[tile % 2], (((0,), (0,)), ((), ())),
                                              preferred_element_type=F32)


def moe_ffn(xn, x, comb, counts, wg, wu, wd, tm=512):
    n, d = x.shape
    n_exp, _, width = wg.shape
    n_tiles = n // tm
    seg_pad = n_exp * (V7X_BF16_ROWS - 1)
    n_slots = -(-(2 * tm + seg_pad) // V7X_LANES) * V7X_LANES
    n_rows = 2 * n + n_tiles * seg_pad + n_exp * (MOE_ROW_TILE - 1)
    n_rows = -(-n_rows // MOE_ROW_TILE) * MOE_ROW_TILE
    tile_off, n16, tail_start, tail_n16, tile_expert, n_used = _moe_layout(counts, tm, n_rows)
    d_x = d + V7X_LANES
    any_spec = pl.BlockSpec(memory_space=pl.ANY)
    sems = pltpu.SemaphoreType.DMA((2, n_exp))

    xs = pl.pallas_call(
        _moe_dispatch_kernel,
        grid_spec=pltpu.PrefetchScalarGridSpec(
            num_scalar_prefetch=5, grid=(n_tiles,),
            in_specs=[pl.BlockSpec((tm, d), lambda i, *_: (i, 0)),
                      pl.BlockSpec((tm, V7X_LANES), lambda i, *_: (i, 0))],
            out_specs=any_spec,
            scratch_shapes=[pltpu.VMEM((2, n_slots, d_x), BF16),
                            pltpu.VMEM((MOE_ROW_TILE, d_x), BF16), sems,
                            pltpu.SemaphoreType.DMA((n_exp,))]),
        out_shape=jax.ShapeDtypeStruct((n_rows, d_x), BF16),
        compiler_params=_params("arbitrary"),
    )(tile_off, n16, tail_start, tail_n16, n_used, xn, comb)

    once = pl.Buffered(1)
    ys = pl.pallas_call(
        _moe_expert_kernel,
        grid_spec=pltpu.PrefetchScalarGridSpec(
            num_scalar_prefetch=2, grid=(n_rows // MOE_ROW_TILE,),
            in_specs=[pl.BlockSpec((MOE_ROW_TILE, d_x), lambda r, te, used: (r, 0)),
                      pl.BlockSpec((1, d, width), lambda r, te, used: (te[r], 0, 0),
                                   pipeline_mode=once),
                      pl.BlockSpec((1, d, width), lambda r, te, used: (te[r], 0, 0),
                                   pipeline_mode=once),
                      pl.BlockSpec((1, width, d), lambda r, te, used: (te[r], 0, 0),
                                   pipeline_mode=once)],
            out_specs=pl.BlockSpec((MOE_ROW_TILE, d), lambda r, te, used: (r, 0))),
        out_shape=jax.ShapeDtypeStruct((n_rows, d), BF16),
        compiler_params=_params("arbitrary"),
    )(tile_expert, n_used, xs, wg, wu, wd)

    return pl.pallas_call(
        _moe_combine_kernel,
        grid_spec=pltpu.PrefetchScalarGridSpec(
            num_scalar_prefetch=2, grid=(n_tiles,),
            in_specs=[any_spec, pl.BlockSpec((tm, V7X_LANES), lambda i, *_: (i, 0)),
                      pl.BlockSpec((tm, d), lambda i, *_: (i, 0))],
            out_specs=pl.BlockSpec((tm, d), lambda i, *_: (i, 0)),
            scratch_shapes=[pltpu.VMEM((2, n_slots, d), BF16), sems]),
        out_shape=jax.ShapeDtypeStruct((n, d), F32),
        compiler_params=_params("arbitrary"),
    )(tile_off, n16, ys, comb, x)


def _slab_gains(q_sb, k_sb, q_dil, k_dil):
    ones = jnp.ones((4, HEAD_DIM), F32)
    rep = lambda g: jnp.broadcast_to(g[None, :], (4, HEAD_DIM))
    return jnp.concatenate([rep(q_sb), rep(k_sb), ones, rep(q_dil), rep(k_dil), ones], axis=0)


def kernel(x, positions, norm_mix, w_in, q_norm_sb, k_norm_sb, q_norm_dil, k_norm_dil, out_norm_sb, out_norm_dil, w_out, norm_ffn, w_gate_dense, w_up_dense, w_down_dense, w_router, w_gate_moe, w_up_moe, w_down_moe):
    batch, seq, d = x.shape
    n = batch * seq
    depth = w_in.shape[0]
    xf = x.reshape(n, d)
    pos_b = jnp.broadcast_to(positions.reshape(n, 1), (n, HEAD_DIM))
    cos, sin = rope_tables(pos_b)
    for i in range(depth):
        gains = _slab_gains(q_norm_sb[i], k_norm_sb[i], q_norm_dil[i], k_norm_dil[i])
        qkv = in_projection(xf, norm_mix[i][None, :], w_in[i].astype(BF16), gains, cos, sin, batch)
        o_sb = stick_breaking_attention(qkv)
        o_dil = dilated_attention(qkv)
        j = i // 2
        mix = (o_sb, o_dil, xf, out_norm_sb[i][None, :], out_norm_dil[i][None, :],
               w_out[i].astype(BF16), norm_ffn[i][None, :])
        if i % 2 == 1:
            xf, xn, comb, cnt = out_projection_router(*mix, w_router[j])
            counts = cnt[::V7X_SUBLANES, :N_EXPERTS].astype(jnp.int32)
            xf = moe_ffn(xn, xf, comb, counts, w_gate_moe[j].astype(BF16),
                         w_up_moe[j].astype(BF16), w_down_moe[j].astype(BF16))
        else:
            xf = out_projection_ffn(*mix, w_gate_dense[j].astype(BF16),
                                    w_up_dense[j].astype(BF16), w_down_dense[j].astype(BF16))
    return xf.reshape(batch, seq, d)
```

```python
import functools
import math

import jax
import jax.numpy as jnp
from jax import lax
from jax.experimental import pallas as pl
from jax.experimental.pallas import tpu as pltpu

HEAD_DIM = 128
N_HEADS_SB = 4
N_HEADS_DIL = 4
N_SLABS = 3 * (N_HEADS_SB + N_HEADS_DIL)
SLAB_Q_SB, SLAB_K_SB, SLAB_V_SB = 0, 4, 8
SLAB_Q_DIL, SLAB_K_DIL, SLAB_V_DIL = 12, 16, 20
DILATED_CONFIGS = ((128, 1), (512, 4), (2048, 16))
DIL_STEPS = 128
N_EXPERTS = 8
ROPE_THETA = 10000.0
EPS = 1e-6
NEG_INF = -1e30
SB_ZERO_WEIGHT_LOG = 112.0

V7X_LANES = 128
V7X_SUBLANES = 8
V7X_BF16_ROWS = 16
V7X_VMEM_LIMIT_BYTES = 56 * 1024 * 1024

F32 = jnp.float32
BF16 = jnp.bfloat16


def _params(*semantics):
    return pltpu.CompilerParams(dimension_semantics=semantics,
                                vmem_limit_bytes=V7X_VMEM_LIMIT_BYTES)


def _dot(a, b):
    return jnp.dot(a, b, preferred_element_type=F32)


def _dot_nt(a, b):
    return lax.dot_general(a, b, (((1,), (1,)), ((), ())), preferred_element_type=F32)


def _rope_kernel(pos_ref, cos_ref, sin_ref):
    lane = lax.broadcasted_iota(jnp.int32, pos_ref.shape, 1)
    half = HEAD_DIM // 2
    pair = (lane % half).astype(F32)
    inv = jnp.exp(pair * (-2.0 * math.log(ROPE_THETA) / HEAD_DIM))
    ang = pos_ref[...].astype(F32) * inv
    cos_ref[...] = jnp.cos(ang)
    s = jnp.sin(ang)
    sin_ref[...] = jnp.where(lane < half, -s, s)


def rope_tables(pos_b, tm=1024):
    n = pos_b.shape[0]
    spec = pl.BlockSpec((tm, HEAD_DIM), lambda i: (i, 0))
    return pl.pallas_call(
        _rope_kernel,
        grid=(n // tm,),
        in_specs=[spec],
        out_specs=[spec, spec],
        out_shape=[jax.ShapeDtypeStruct((n, HEAD_DIM), F32)] * 2,
        compiler_params=_params("parallel"),
    )(pos_b)


def _inproj_kernel(x_ref, g_ref, w_ref, gains_ref, cos_ref, sin_ref, o_ref):
    scale = 1.0 / math.sqrt(HEAD_DIM)
    group = 4
    tm = x_ref.shape[0]
    for rows in (slice(0, tm // 2), slice(tm // 2, tm)):
        x = x_ref[rows, :]
        ms = jnp.mean(x * x, axis=-1, keepdims=True)
        xn = (x * lax.rsqrt(ms + EPS) * g_ref[...]).astype(BF16)
        cos = cos_ref[rows, :]
        sin = sin_ref[rows, :]
        for j0 in range(0, N_SLABS, group):
            h = _dot(xn, w_ref[:, j0 * HEAD_DIM:(j0 + group) * HEAD_DIM])
            for j in range(j0, j0 + group):
                y = h[:, (j - j0) * HEAD_DIM:(j - j0 + 1) * HEAD_DIM]
                is_v = SLAB_V_SB <= j < SLAB_Q_DIL or j >= SLAB_V_DIL
                if not is_v:
                    ms_h = jnp.mean(y * y, axis=-1, keepdims=True)
                    y = y * lax.rsqrt(ms_h + EPS) * gains_ref[j:j + 1, :]
                    if j >= SLAB_Q_DIL:
                        y = y * cos + pltpu.roll(y, HEAD_DIM // 2, 1) * sin
                    if j < SLAB_K_SB or SLAB_Q_DIL <= j < SLAB_K_DIL:
                        y = y * scale
                o_ref[0, j, rows, :] = y.astype(BF16)


def in_projection(x, g_norm, w_in, gains, cos, sin, batch, tm=512):
    n, d = x.shape
    s = n // batch
    d_in = w_in.shape[1]
    nt = s // tm
    return pl.pallas_call(
        _inproj_kernel,
        grid=(batch, nt),
        in_specs=[
            pl.BlockSpec((tm, d), lambda b, i: (b * nt + i, 0)),
            pl.BlockSpec((1, d), lambda b, i: (0, 0)),
            pl.BlockSpec((d, d_in), lambda b, i: (0, 0)),
            pl.BlockSpec((N_SLABS, HEAD_DIM), lambda b, i: (0, 0)),
            pl.BlockSpec((tm, HEAD_DIM), lambda b, i: (b * nt + i, 0)),
            pl.BlockSpec((tm, HEAD_DIM), lambda b, i: (b * nt + i, 0)),
        ],
        out_specs=pl.BlockSpec((1, N_SLABS, tm, HEAD_DIM), lambda b, i: (b, 0, i, 0)),
        out_shape=jax.ShapeDtypeStruct((batch, N_SLABS, s, HEAD_DIM), BF16),
        compiler_params=_params("parallel", "parallel"),
    )(x, g_norm, w_in, gains, cos, sin)


def _softplus(z):
    return jnp.maximum(z, 0.0) + jnp.log(1.0 + jnp.exp(-jnp.abs(z)))


def _sb_kernel(q_ref, k_ref, v_ref, o_ref, *, blk, nsub):
    i = pl.program_id(2)
    row = lax.broadcasted_iota(jnp.int32, (blk, blk), 0)
    col = lax.broadcasted_iota(jnp.int32, (blk, blk), 1)
    incl = (row >= col).astype(BF16)
    strict = col < row

    def block(q, c, later, acc, diagonal):
        start = pl.multiple_of(jnp.maximum(c, 0) * blk, blk)
        kb = k_ref[0, 0, pl.ds(start, blk), :]
        vb = v_ref[0, 0, pl.ds(start, blk), :]
        z = _dot_nt(q, kb)
        sp = _softplus(z)
        if diagonal:
            sp = jnp.where(strict, sp, 0.0)
        r = _dot(sp.astype(BF16), incl)
        a = jnp.exp(z - r - later)
        total = r[:, :1]
        if diagonal:
            a = jnp.where(strict, a, 0.0)
        else:
            a = jnp.where(c >= 0, a, 0.0)
            total = jnp.where(c >= 0, total, 0.0)
        return later + total, acc + _dot(a.astype(BF16), vb)

    qs = [q_ref[0, 0, j * blk:(j + 1) * blk, :] for j in range(nsub)]
    own = [i * nsub + j for j in range(nsub)]
    state = [block(qs[j], own[j], jnp.zeros((blk, 1), F32), jnp.zeros((blk, HEAD_DIM), F32), True)
             for j in range(nsub)]

    def unfinished(back, laters):
        flags = [jnp.logical_and(own[j] - back >= 0, jnp.min(laters[j]) < SB_ZERO_WEIGHT_LOG)
                 for j in range(nsub)]
        return functools.reduce(jnp.logical_or, flags)

    def cond(carry):
        return carry[1]

    def body(carry):
        back, _, laters, accs = carry
        new = [block(qs[j], own[j] - back, laters[j], accs[j], False) for j in range(nsub)]
        laters, accs = [s[0] for s in new], [s[1] for s in new]
        return back + 1, unfinished(back + 1, laters), laters, accs

    state = [block(qs[j], own[j] - 1, state[j][0], state[j][1], False) for j in range(nsub)]
    laters, accs = [s[0] for s in state], [s[1] for s in state]
    _, _, _, accs = lax.while_loop(cond, body, (2, unfinished(2, laters), laters, accs))
    for j in range(nsub):
        o_ref[0, j * blk:(j + 1) * blk, :] = accs[j].astype(o_ref.dtype)


def stick_breaking_attention(qkv, blk=256, nsub=4):
    batch, _, s, dh = qkv.shape
    nh = N_HEADS_SB
    o = pl.pallas_call(
        functools.partial(_sb_kernel, blk=blk, nsub=nsub),
        grid=(batch, nh, s // (blk * nsub)),
        in_specs=[
            pl.BlockSpec((1, 1, blk * nsub, dh), lambda b, h, i: (b, SLAB_Q_SB + h, i, 0)),
            pl.BlockSpec((1, 1, s, dh), lambda b, h, i: (b, SLAB_K_SB + h, 0, 0)),
            pl.BlockSpec((1, 1, s, dh), lambda b, h, i: (b, SLAB_V_SB + h, 0, 0)),
        ],
        out_specs=pl.BlockSpec((1, blk * nsub, dh), lambda b, h, i: (b, i, h)),
        out_shape=jax.ShapeDtypeStruct((batch, s, nh * dh), BF16),
        compiler_params=_params("parallel", "parallel", "arbitrary"),
    )(qkv, qkv, qkv)
    return o.reshape(batch * s, nh * dh)


DIL_GROUP = 4


def _dil_kernel(q_ref, k_ref, v_ref, o_ref, kf_ref, vf_ref, kg_ref, vg_ref, qf_ref, qg_ref,
                po_ref, pm_ref, pl_ref, of_ref):
    i = pl.program_id(2)
    st = DIL_STEPS
    grp = DIL_GROUP
    span, dh = qf_ref.shape
    seq = kf_ref.shape[0]
    chunk = 512

    @pl.when(i == 0)
    def _():
        def conv(c, carry):
            rows = pl.ds(pl.multiple_of(c * chunk, chunk), chunk)
            kf_ref[rows, :] = k_ref[0, 0, rows, :].astype(F32)
            vf_ref[rows, :] = v_ref[0, 0, rows, :].astype(F32)
            return carry
        lax.fori_loop(0, seq // chunk, conv, 0)

        def regroup(c, carry):
            for r in range(grp):
                src = pl.ds(c * chunk * grp + r, chunk, stride=grp)
                dst = pl.ds(pl.multiple_of(r * (seq // grp) + c * chunk, chunk), chunk)
                kg_ref[dst, :] = kf_ref[src, :]
                vg_ref[dst, :] = vf_ref[src, :]
            return carry
        lax.fori_loop(0, seq // grp // chunk, regroup, 0)

    qf_ref[...] = q_ref[0, 0].astype(F32)
    for r in range(grp):
        qg_ref[r * (span // grp):(r + 1) * (span // grp), :] = (
            qf_ref[pl.ds(r, span // grp, stride=grp), :])

    diff = (lax.broadcasted_iota(jnp.int32, (st, 2 * st), 1)
            - lax.broadcasted_iota(jnp.int32, (st, 2 * st), 0))
    bias_normal = jnp.where((diff >= 0) & (diff <= st), 0.0, NEG_INF)
    bias_first = jnp.where(diff <= 0, 0.0, NEG_INF)
    ones = jnp.ones((2 * st, dh), BF16)

    def strided(start, size, stride):
        return pl.ds(start, size) if stride == 1 else pl.ds(start, size, stride=stride)

    for c, (_, dil) in enumerate(DILATED_CONFIGS):
        g = grp if dil % grp == 0 else 1
        q_buf, k_buf, v_buf = (qg_ref, kg_ref, vg_ref) if g == grp else (qf_ref, kf_ref, vf_ref)

        def unit(u, carry, c=c, dil=dil, g=g, q_buf=q_buf, k_buf=k_buf, v_buf=v_buf):
            inner = dil // g
            r, blk = u % dil, u // dil
            group_row = (r % g) * (span // g)
            q_start = group_row + blk * st * inner + r // g
            block = i * (span // g) + blk * st * inner
            first = block == 0
            w_start = (r % g) * (seq // g) + jnp.where(first, 0, block - st * inner) + r // g
            q = q_buf[strided(q_start, st, inner), :].astype(BF16)
            kw = k_buf[strided(w_start, 2 * st, inner), :].astype(BF16)
            vw = v_buf[strided(w_start, 2 * st, inner), :].astype(BF16)
            s = _dot_nt(q, kw) + jnp.where(first, bias_first, bias_normal)
            m = jnp.max(s, axis=-1, keepdims=True)
            p = jnp.exp(s - m).astype(BF16)
            o_den = _dot(p, jnp.concatenate([vw, ones], axis=1))
            rows = strided(q_start, st, inner)
            po_ref[c, rows, :] = o_den[:, :dh]
            pl_ref[c, rows, :] = o_den[:, dh:]
            pm_ref[c, rows, :] = jnp.broadcast_to(m, (st, dh))
            return carry

        lax.fori_loop(0, span // st, unit, 0, unroll=True)

    def merge(c, carry):
        rows_g = pl.ds(pl.multiple_of(c * st, st), st)
        per_group = span // grp // st
        rows_n = pl.ds((c % per_group) * st * grp + c // per_group, st, stride=grp)
        rows = [rows_g if dil % grp == 0 else rows_n for _, dil in DILATED_CONFIGS]
        ms = [pm_ref[b, rows[b], :] for b in range(len(rows))]
        m = functools.reduce(jnp.maximum, ms)
        ws = [jnp.exp(mb - m) for mb in ms]
        den = sum(w * pl_ref[b, rows[b], :] for b, w in enumerate(ws))
        num = sum(w * po_ref[b, rows[b], :] for b, w in enumerate(ws))
        of_ref[rows_n, :] = num / den
        return carry

    lax.fori_loop(0, span // st, merge, 0, unroll=2)
    o_ref[0] = of_ref[...].astype(o_ref.dtype)


def dilated_attention(qkv):
    batch, _, s, dh = qkv.shape
    nh = N_HEADS_DIL
    span = DIL_STEPS * max(dil for _, dil in DILATED_CONFIGS)
    assert s % span == 0 and s >= 2 * span
    n_cfg = len(DILATED_CONFIGS)
    o = pl.pallas_call(
        _dil_kernel,
        grid=(batch, nh, s // span),
        in_specs=[
            pl.BlockSpec((1, 1, span, dh), lambda b, h, i: (b, SLAB_Q_DIL + h, i, 0)),
            pl.BlockSpec((1, 1, s, dh), lambda b, h, i: (b, SLAB_K_DIL + h, 0, 0)),
            pl.BlockSpec((1, 1, s, dh), lambda b, h, i: (b, SLAB_V_DIL + h, 0, 0)),
        ],
        out_specs=pl.BlockSpec((1, span, dh), lambda b, h, i: (b, i, h)),
        out_shape=jax.ShapeDtypeStruct((batch, s, nh * dh), BF16),
        scratch_shapes=(
            [pltpu.VMEM((s, dh), F32)] * 4
            + [pltpu.VMEM((span, dh), F32)] * 2
            + [pltpu.VMEM((n_cfg, span, dh), F32)] * 3
            + [pltpu.VMEM((span, dh), F32)]),
        compiler_params=_params("parallel", "parallel", "arbitrary"),
    )(qkv, qkv, qkv)
    return o.reshape(batch * s, nh * dh)


def _rms(y, g):
    return y * lax.rsqrt(jnp.mean(y * y, axis=-1, keepdims=True) + EPS) * g


def _mix_out(osb_ref, odil_ref, x_ref, gsb_ref, gdil_ref, w_ref, gffn_ref):
    d_sb = osb_ref.shape[1]
    y_sb = _rms(osb_ref[...].astype(F32), gsb_ref[...]).astype(BF16)
    y_dil = _rms(odil_ref[...].astype(F32), gdil_ref[...]).astype(BF16)
    x = x_ref[...] + _dot(y_sb, w_ref[:d_sb, :]) + _dot(y_dil, w_ref[d_sb:, :])
    return x, _rms(x, gffn_ref[...])


def _outproj_router_kernel(osb_ref, odil_ref, x_ref, gsb_ref, gdil_ref, w_ref, gffn_ref, wr_ref,
                           xo_ref, xn_ref, comb_ref, cnt_ref):
    tm = x_ref.shape[0]
    x, xn = _mix_out(osb_ref, odil_ref, x_ref, gsb_ref, gdil_ref, w_ref, gffn_ref)
    xo_ref[...] = x
    xn_hi = xn.astype(BF16)
    xn_ref[...] = xn_hi

    n_e = N_EXPERTS
    xn_lo = (xn - xn_hi.astype(F32)).astype(BF16)
    wr = wr_ref[...]
    wr_hi = wr.astype(BF16).astype(F32)
    lane_w = lax.broadcasted_iota(jnp.int32, wr.shape, 1)
    rhs = jnp.where(lane_w < n_e, wr_hi, wr - wr_hi).astype(BF16)
    prod = _dot(jnp.concatenate([xn_hi, xn_lo], axis=0), rhs)
    hi_rows = prod[:tm]
    logits = hi_rows + pltpu.roll(hi_rows, V7X_LANES - n_e, 1) + prod[tm:]
    lane = lax.broadcasted_iota(jnp.int32, comb_ref.shape, 1)
    logits = jnp.where(lane < n_e, logits, -jnp.inf)
    top1 = jnp.max(logits, axis=-1, keepdims=True)
    idx1 = jnp.min(jnp.where(logits == top1, lane, V7X_LANES), axis=-1, keepdims=True)
    rest = jnp.where(lane == idx1, -jnp.inf, logits)
    top2 = jnp.max(rest, axis=-1, keepdims=True)
    idx2 = jnp.min(jnp.where(rest == top2, lane, V7X_LANES), axis=-1, keepdims=True)
    t = jnp.exp(top2 - top1)
    gate1 = 1.0 / (1.0 + t)
    gate2 = t / (1.0 + t)
    comb = jnp.where(lane == idx1, gate1, 0.0) + jnp.where(lane == idx2, gate2, 0.0)
    comb_ref[...] = comb
    routed = jnp.sum(jnp.where(comb > 0.0, 1.0, 0.0), axis=0, keepdims=True)
    cnt_ref[...] = jnp.broadcast_to(routed, cnt_ref.shape)


def _mix_specs(tm, d, d_sb, d_dil, index):
    row = lambda width: pl.BlockSpec((tm, width), lambda *g: (index(*g), 0))
    const = lambda shape: pl.BlockSpec(shape, lambda *g: (0, 0))
    return [row(d_sb), row(d_dil), row(d), const((1, d_sb)), const((1, d_dil)),
            const((d_sb + d_dil, d)), const((1, d))]


def out_projection_router(o_sb, o_dil, x, g_sb, g_dil, w_out, g_ffn, w_router, tm=512):
    n, d = x.shape
    n_e = w_router.shape[1]
    w_router_2x = jnp.pad(jnp.concatenate([w_router, w_router], axis=1),
                         ((0, 0), (0, V7X_LANES - 2 * n_e)))
    row = lambda width: pl.BlockSpec((tm, width), lambda i: (i, 0))
    return pl.pallas_call(
        _outproj_router_kernel,
        grid=(n // tm,),
        in_specs=_mix_specs(tm, d, o_sb.shape[1], o_dil.shape[1], lambda i: i) + [
            pl.BlockSpec(w_router_2x.shape, lambda i: (0, 0))],
        out_specs=[row(d), row(d), row(V7X_LANES),
                   pl.BlockSpec((V7X_SUBLANES, V7X_LANES), lambda i: (i, 0))],
        out_shape=[jax.ShapeDtypeStruct((n, d), F32), jax.ShapeDtypeStruct((n, d), BF16),
                   jax.ShapeDtypeStruct((n, V7X_LANES), F32),
                   jax.ShapeDtypeStruct((n // tm * V7X_SUBLANES, V7X_LANES), F32)],
        compiler_params=_params("parallel"),
    )(o_sb, o_dil, x, g_sb, g_dil, w_out, g_ffn, w_router_2x)


def _outproj_ffn_kernel(osb_ref, odil_ref, x_ref, gsb_ref, gdil_ref, w_ref, gffn_ref,
                        wg_ref, wu_ref, wd_ref, o_ref, xn_ref):
    @pl.when(pl.program_id(1) == 0)
    def _():
        x, xn = _mix_out(osb_ref, odil_ref, x_ref, gsb_ref, gdil_ref, w_ref, gffn_ref)
        o_ref[...] = x
        xn_ref[...] = xn.astype(BF16)

    xn = xn_ref[...]
    gate = _dot(xn, wg_ref[...])
    up = _dot(xn, wu_ref[...])
    hid = gate * (1.0 / (1.0 + jnp.exp(-gate))) * up
    o_ref[...] += _dot(hid.astype(BF16), wd_ref[...])


def out_projection_ffn(o_sb, o_dil, x, g_sb, g_dil, w_out, g_ffn, wg, wu, wd, tm=512, tf=512):
    n, d = x.shape
    width = wg.shape[1]
    assert width % tf == 0
    return pl.pallas_call(
        _outproj_ffn_kernel,
        grid=(n // tm, width // tf),
        in_specs=_mix_specs(tm, d, o_sb.shape[1], o_dil.shape[1], lambda i, f: i) + [
            pl.BlockSpec((d, tf), lambda i, f: (0, f)),
            pl.BlockSpec((d, tf), lambda i, f: (0, f)),
            pl.BlockSpec((tf, d), lambda i, f: (f, 0))],
        out_specs=pl.BlockSpec((tm, d), lambda i, f: (i, 0)),
        out_shape=jax.ShapeDtypeStruct((n, d), F32),
        scratch_shapes=[pltpu.VMEM((tm, d), BF16)],
        compiler_params=_params("parallel", "arbitrary"),
    )(o_sb, o_dil, x, g_sb, g_dil, w_out, g_ffn, wg, wu, wd)


MOE_ROW_TILE = 512
MOE_SEG_BITS = (32, 16, 8, 4, 2, 1)


def _moe_layout(counts, tm, n_rows):
    n_exp = counts.shape[1]
    n16 = (counts + V7X_BF16_ROWS - 1) // V7X_BF16_ROWS
    rows = n16 * V7X_BF16_ROWS
    filled = jnp.sum(rows, axis=0)
    region = (filled + MOE_ROW_TILE - 1) // MOE_ROW_TILE * MOE_ROW_TILE
    ends = jnp.cumsum(region)
    tile_off = (ends - region)[None, :] + jnp.cumsum(rows, axis=0) - rows
    tail_start = ends - region + filled
    tail_n16 = (region - filled) // V7X_BF16_ROWS
    row_start = jnp.arange(n_rows // MOE_ROW_TILE, dtype=jnp.int32) * MOE_ROW_TILE
    tile_expert = jnp.minimum(jnp.sum(row_start[:, None] >= ends[None, :], axis=1), n_exp - 1)
    n_used = ends[-1:] // MOE_ROW_TILE
    i32 = lambda a: a.astype(jnp.int32)
    return (i32(tile_off).reshape(-1), i32(n16).reshape(-1), i32(tail_start), i32(tail_n16),
            i32(tile_expert), i32(n_used))


def _moe_tile_segments(n16_ref, tile):
    segs, start = [], jnp.int32(0)
    for e in range(N_EXPERTS):
        pieces = n16_ref[tile * N_EXPERTS + e]
        segs.append((start, pieces))
        start = start + pieces * V7X_BF16_ROWS
    return segs


def _moe_tile_slots(comb, segs, n_slots):
    tm = comb.shape[0]
    rows_e = V7X_BF16_ROWS
    comb_t = comb.T[:rows_e, :]
    routed = comb_t > 0.0
    before = (lax.broadcasted_iota(jnp.int32, (tm, tm), 0)
              < lax.broadcasted_iota(jnp.int32, (tm, tm), 1))
    rank = _dot(jnp.where(routed, 1.0, 0.0).astype(BF16),
                jnp.where(before, 1.0, 0.0).astype(BF16))
    sub = lax.broadcasted_iota(jnp.int32, (rows_e, tm), 0)
    seg_col = jnp.zeros((rows_e, tm), F32)
    for e, (start, _) in enumerate(segs):
        seg_col = jnp.where(sub == e, start.astype(F32), seg_col)
    slot = jnp.where(routed, rank + seg_col, -1.0)
    n_routed = jnp.sum(jnp.where(routed, 1.0, 0.0), axis=0, keepdims=True)
    slot_hi = jnp.max(slot, axis=0, keepdims=True)
    slot_lo = jnp.where(n_routed >= 2.0,
                        jnp.sum(jnp.where(routed, slot, 0.0), axis=0, keepdims=True) - slot_hi,
                        -1.0)
    gate_hi = jnp.sum(jnp.where(slot == slot_hi, comb_t, 0.0), axis=0, keepdims=True)
    gate_lo = jnp.sum(comb_t, axis=0, keepdims=True) - gate_hi
    s_idx = lax.broadcasted_iota(jnp.int32, (n_slots, tm), 0).astype(F32)
    hit_hi = s_idx == slot_hi
    hit_lo = s_idx == slot_lo
    onehot = jnp.where(hit_hi, 1.0, jnp.where(hit_lo, 1.0, 0.0)).astype(BF16)
    gate_slot = jnp.sum(jnp.where(hit_hi, gate_hi, jnp.where(hit_lo, gate_lo, 0.0)),
                        axis=1, keepdims=True)
    return onehot, gate_slot


def _piece_copies(pieces, local_rows, hbm_ref, hbm_start, sem, to_hbm):
    copies = []
    for bit in MOE_SEG_BITS:
        done = (pieces // (2 * bit)) * (2 * bit) * V7X_BF16_ROWS
        rows = bit * V7X_BF16_ROWS
        local = local_rows(done, rows)
        remote = hbm_ref.at[pl.ds(pl.multiple_of(hbm_start + done, V7X_BF16_ROWS), rows)]
        src, dst = (local, remote) if to_hbm else (remote, local)
        copies.append(((pieces & bit) != 0, pltpu.make_async_copy(src, dst, sem)))
    return copies


def _segment_copies(buf_ref, hbm_ref, off_ref, n16_ref, tile, sems, to_hbm):
    slot = tile % 2
    copies = []
    for e, (seg_start, pieces) in enumerate(_moe_tile_segments(n16_ref, tile)):
        local_rows = lambda done, rows, seg_start=seg_start: buf_ref.at[
            slot, pl.ds(pl.multiple_of(seg_start + done, V7X_BF16_ROWS), rows)]
        copies += _piece_copies(pieces, local_rows, hbm_ref, off_ref[tile * N_EXPERTS + e],
                                sems.at[slot, e], to_hbm)
    return copies


def _start_all(copies):
    for pred, cp in copies:
        pl.when(pred)(cp.start)


def _wait_all(copies):
    for pred, cp in copies:
        pl.when(pred)(cp.wait)


def _moe_dispatch_kernel(off_ref, n16_ref, tail_start_ref, tail_n16_ref, used_ref, xn_ref,
                         comb_ref, xs_ref, buf_ref, zero_ref, sems, tail_sems):
    tile = pl.program_id(0)
    slot = tile % 2
    d = xn_ref.shape[1]
    n_slots = buf_ref.shape[1]
    segs = _moe_tile_segments(n16_ref, tile)
    onehot, gate_slot = _moe_tile_slots(comb_ref[...], segs, n_slots)
    buf_ref[slot, :, :d] = _dot(onehot, xn_ref[...]).astype(BF16)
    g_hi = gate_slot.astype(BF16).astype(F32)
    g_lo = gate_slot - g_hi
    lane = lax.broadcasted_iota(jnp.int32, (n_slots, V7X_LANES), 1)
    buf_ref[slot, :, d:] = jnp.where(lane == 0, g_hi,
                                     jnp.where(lane == 1, g_lo, 0.0)).astype(BF16)
    copies = _segment_copies(buf_ref, xs_ref, off_ref, n16_ref, tile, sems, to_hbm=True)
    _start_all(copies)

    @pl.when(tile > 0)
    def _():
        _wait_all(_segment_copies(buf_ref, xs_ref, off_ref, n16_ref, tile - 1, sems, to_hbm=True))

    @pl.when(tile == pl.num_programs(0) - 1)
    def _():
        _wait_all(copies)
        zero_ref[...] = jnp.zeros_like(zero_ref)
        tails = []
        for e in range(N_EXPERTS):
            tails += _piece_copies(tail_n16_ref[e], lambda done, rows: zero_ref.at[pl.ds(0, rows)],
                                   xs_ref, tail_start_ref[e], tail_sems.at[e], to_hbm=True)
        _start_all(tails)
        _wait_all(tails)

        def zero_row_tile(r, carry):
            rows = pl.ds(pl.multiple_of(r * MOE_ROW_TILE, MOE_ROW_TILE), MOE_ROW_TILE)
            cp = pltpu.make_async_copy(zero_ref, xs_ref.at[rows], tail_sems.at[0])
            cp.start()
            cp.wait()
            return carry
        lax.fori_loop(used_ref[0], xs_ref.shape[0] // MOE_ROW_TILE, zero_row_tile, 0)


def _moe_expert_kernel(te_ref, used_ref, xs_ref, wg_ref, wu_ref, wd_ref, ys_ref):
    d = wg_ref.shape[1]
    live = pl.program_id(0) < used_ref[0]

    @pl.when(live)
    def _():
        xs = xs_ref[:, :d]
        weight = xs_ref[:, d:d + 1].astype(F32) + xs_ref[:, d + 1:d + 2].astype(F32)
        gate = _dot(xs, wg_ref[0])
        up = _dot(xs, wu_ref[0])
        hid = gate * (1.0 / (1.0 + jnp.exp(-gate))) * up * weight
        ys_ref[...] = _dot(hid.astype(BF16), wd_ref[0]).astype(BF16)

    @pl.when(jnp.logical_not(live))
    def _():
        ys_ref[...] = jnp.zeros_like(ys_ref)


def _moe_combine_kernel(off_ref, n16_ref, ys_ref, comb_ref, x_ref, o_ref, buf_ref, sems):
    tile = pl.program_id(0)
    n_slots = buf_ref.shape[1]
    fetch = lambda t: _segment_copies(buf_ref, ys_ref, off_ref, n16_ref, t, sems, to_hbm=False)

    @pl.when(tile == 0)
    def _():
        buf_ref[...] = jnp.zeros_like(buf_ref)
        _start_all(fetch(tile))

    @pl.when(tile + 1 < pl.num_programs(0))
    def _():
        _start_all(fetch(tile + 1))

    onehot, _ = _moe_tile_slots(comb_ref[...], _moe_tile_segments(n16_ref, tile), n_slots)
    _wait_all(fetch(tile))
    o_ref[...] = x_ref[...] + lax.dot_general(onehot, buf_ref[tile % 2], (((0,), (0,)), ((), ())),
                                              preferred_element_type=F32)


def moe_ffn(xn, x, comb, counts, wg, wu, wd, tm=512):
    n, d = x.shape
    n_exp, _, width = wg.shape
    n_tiles = n // tm
    seg_pad = n_exp * (V7X_BF16_ROWS - 1)
    n_slots = -(-(2 * tm + seg_pad) // V7X_LANES) * V7X_LANES
    n_rows = 2 * n + n_tiles * seg_pad + n_exp * (MOE_ROW_TILE - 1)
    n_rows = -(-n_rows // MOE_ROW_TILE) * MOE_ROW_TILE
    tile_off, n16, tail_start, tail_n16, tile_expert, n_used = _moe_layout(counts, tm, n_rows)
    d_x = d + V7X_LANES
    any_spec = pl.BlockSpec(memory_space=pl.ANY)
    sems = pltpu.SemaphoreType.DMA((2, n_exp))

    xs = pl.pallas_call(
        _moe_dispatch_kernel,
        grid_spec=pltpu.PrefetchScalarGridSpec(
            num_scalar_prefetch=5, grid=(n_tiles,),
            in_specs=[pl.BlockSpec((tm, d), lambda i, *_: (i, 0)),
                      pl.BlockSpec((tm, V7X_LANES), lambda i, *_: (i, 0))],
            out_specs=any_spec,
            scratch_shapes=[pltpu.VMEM((2, n_slots, d_x), BF16),
                            pltpu.VMEM((MOE_ROW_TILE, d_x), BF16), sems,
                            pltpu.SemaphoreType.DMA((n_exp,))]),
        out_shape=jax.ShapeDtypeStruct((n_rows, d_x), BF16),
        compiler_params=_params("arbitrary"),
    )(tile_off, n16, tail_start, tail_n16, n_used, xn, comb)

    once = pl.Buffered(1)
    ys = pl.pallas_call(
        _moe_expert_kernel,
        grid_spec=pltpu.PrefetchScalarGridSpec(
            num_scalar_prefetch=2, grid=(n_rows // MOE_ROW_TILE,),
            in_specs=[pl.BlockSpec((MOE_ROW_TILE, d_x), lambda r, te, used: (r, 0)),
                      pl.BlockSpec((1, d, width), lambda r, te, used: (te[r], 0, 0),
                                   pipeline_mode=once),
                      pl.BlockSpec((1, d, width), lambda r, te, used: (te[r], 0, 0),
                                   pipeline_mode=once),
                      pl.BlockSpec((1, width, d), lambda r, te, used: (te[r], 0, 0),
                                   pipeline_mode=once)],
            out_specs=pl.BlockSpec((MOE_ROW_TILE, d), lambda r, te, used: (r, 0))),
        out_shape=jax.ShapeDtypeStruct((n_rows, d), BF16),
        compiler_params=_params("arbitrary"),
    )(tile_expert, n_used, xs, wg, wu, wd)

    return pl.pallas_call(
        _moe_combine_kernel,
        grid_spec=pltpu.PrefetchScalarGridSpec(
            num_scalar_prefetch=2, grid=(n_tiles,),
            in_specs=[any_spec, pl.BlockSpec((tm, V7X_LANES), lambda i, *_: (i, 0)),
                      pl.BlockSpec((tm, d), lambda i, *_: (i, 0))],
            out_specs=pl.BlockSpec((tm, d), lambda i, *_: (i, 0)),
            scratch_shapes=[pltpu.VMEM((2, n_slots, d), BF16), sems]),
        out_shape=jax.ShapeDtypeStruct((n, d), F32),
        compiler_params=_params("arbitrary"),
    )(tile_off, n16, ys, comb, x)


def _slab_gains(q_sb, k_sb, q_dil, k_dil):
    ones = jnp.ones((4, HEAD_DIM), F32)
    rep = lambda g: jnp.broadcast_to(g[None, :], (4, HEAD_DIM))
    return jnp.concatenate([rep(q_sb), rep(k_sb), ones, rep(q_dil), rep(k_dil), ones], axis=0)


def kernel(x, positions, norm_mix, w_in, q_norm_sb, k_norm_sb, q_norm_dil, k_norm_dil, out_norm_sb, out_norm_dil, w_out, norm_ffn, w_gate_dense, w_up_dense, w_down_dense, w_router, w_gate_moe, w_up_moe, w_down_moe):
    batch, seq, d = x.shape
    n = batch * seq
    depth = w_in.shape[0]
    xf = x.reshape(n, d)
    pos_b = jnp.broadcast_to(positions.reshape(n, 1), (n, HEAD_DIM))
    cos, sin = rope_tables(pos_b)
    for i in range(depth):
        gains = _slab_gains(q_norm_sb[i], k_norm_sb[i], q_norm_dil[i], k_norm_dil[i])
        qkv = in_projection(xf, norm_mix[i][None, :], w_in[i].astype(BF16), gains, cos, sin, batch)
        o_sb = stick_breaking_attention(qkv)
        o_dil = dilated_attention(qkv)
        j = i // 2
        mix = (o_sb, o_dil, xf, out_norm_sb[i][None, :], out_norm_dil[i][None, :],
               w_out[i].astype(BF16), norm_ffn[i][None, :])
        if i % 2 == 1:
            xf, xn, comb, cnt = out_projection_router(*mix, w_router[j])
            counts = cnt[::V7X_SUBLANES, :N_EXPERTS].astype(jnp.int32)
            xf = moe_ffn(xn, xf, comb, counts, w_gate_moe[j].astype(BF16),
                         w_up_moe[j].astype(BF16), w_down_moe[j].astype(BF16))
        else:
            xf = out_projection_ffn(*mix, w_gate_dense[j].astype(BF16),
                                    w_up_dense[j].astype(BF16), w_down_dense[j].astype(BF16))
    return xf.reshape(batch, seq, d)
```

```python
import functools
import math

import jax
import jax.numpy as jnp
from jax import lax
from jax.experimental import pallas as pl
from jax.experimental.pallas import tpu as pltpu

HEAD_DIM = 128
N_HEADS_SB = 4
N_HEADS_DIL = 4
N_SLABS = 3 * (N_HEADS_SB + N_HEADS_DIL)
SLAB_Q_SB, SLAB_K_SB, SLAB_V_SB = 0, 4, 8
SLAB_Q_DIL, SLAB_K_DIL, SLAB_V_DIL = 12, 16, 20
DILATED_CONFIGS = ((128, 1), (512, 4), (2048, 16))
DIL_STEPS = 128
N_EXPERTS = 8
ROPE_THETA = 10000.0
EPS = 1e-6
NEG_INF = -1e30
SB_ZERO_WEIGHT_LOG = 112.0

V7X_LANES = 128
V7X_SUBLANES = 8
V7X_BF16_ROWS = 16
V7X_VMEM_LIMIT_BYTES = 56 * 1024 * 1024

F32 = jnp.float32
BF16 = jnp.bfloat16


def _params(*semantics):
    return pltpu.CompilerParams(dimension_semantics=semantics,
                                vmem_limit_bytes=V7X_VMEM_LIMIT_BYTES)


def _dot(a, b):
    return jnp.dot(a, b, preferred_element_type=F32)


def _dot_nt(a, b):
    return lax.dot_general(a, b, (((1,), (1,)), ((), ())), preferred_element_type=F32)


def _rope_kernel(pos_ref, cos_ref, sin_ref):
    lane = lax.broadcasted_iota(jnp.int32, pos_ref.shape, 1)
    half = HEAD_DIM // 2
    pair = (lane % half).astype(F32)
    inv = jnp.exp(pair * (-2.0 * math.log(ROPE_THETA) / HEAD_DIM))
    ang = pos_ref[...].astype(F32) * inv
    cos_ref[...] = jnp.cos(ang)
    s = jnp.sin(ang)
    sin_ref[...] = jnp.where(lane < half, -s, s)


def rope_tables(pos_b, tm=1024):
    n = pos_b.shape[0]
    spec = pl.BlockSpec((tm, HEAD_DIM), lambda i: (i, 0))
    return pl.pallas_call(
        _rope_kernel,
        grid=(n // tm,),
        in_specs=[spec],
        out_specs=[spec, spec],
        out_shape=[jax.ShapeDtypeStruct((n, HEAD_DIM), F32)] * 2,
        compiler_params=_params("parallel"),
    )(pos_b)


def _inproj_kernel(x_ref, g_ref, w_ref, gains_ref, cos_ref, sin_ref, o_ref):
    scale = 1.0 / math.sqrt(HEAD_DIM)
    group = 4
    tm = x_ref.shape[0]
    for rows in (slice(0, tm // 2), slice(tm // 2, tm)):
        x = x_ref[rows, :]
        ms = jnp.mean(x * x, axis=-1, keepdims=True)
        xn = (x * lax.rsqrt(ms + EPS) * g_ref[...]).astype(BF16)
        cos = cos_ref[rows, :]
        sin = sin_ref[rows, :]
        for j0 in range(0, N_SLABS, group):
            h = _dot(xn, w_ref[:, j0 * HEAD_DIM:(j0 + group) * HEAD_DIM])
            for j in range(j0, j0 + group):
                y = h[:, (j - j0) * HEAD_DIM:(j - j0 + 1) * HEAD_DIM]
                is_v = SLAB_V_SB <= j < SLAB_Q_DIL or j >= SLAB_V_DIL
                if not is_v:
                    ms_h = jnp.mean(y * y, axis=-1, keepdims=True)
                    y = y * lax.rsqrt(ms_h + EPS) * gains_ref[j:j + 1, :]
                    if j >= SLAB_Q_DIL:
                        y = y * cos + pltpu.roll(y, HEAD_DIM // 2, 1) * sin
                    if j < SLAB_K_SB or SLAB_Q_DIL <= j < SLAB_K_DIL:
                        y = y * scale
                o_ref[0, j, rows, :] = y.astype(BF16)


def in_projection(x, g_norm, w_in, gains, cos, sin, batch, tm=512):
    n, d = x.shape
    s = n // batch
    d_in = w_in.shape[1]
    nt = s // tm
    return pl.pallas_call(
        _inproj_kernel,
        grid=(batch, nt),
        in_specs=[
            pl.BlockSpec((tm, d), lambda b, i: (b * nt + i, 0)),
            pl.BlockSpec((1, d), lambda b, i: (0, 0)),
            pl.BlockSpec((d, d_in), lambda b, i: (0, 0)),
            pl.BlockSpec((N_SLABS, HEAD_DIM), lambda b, i: (0, 0)),
            pl.BlockSpec((tm, HEAD_DIM), lambda b, i: (b * nt + i, 0)),
            pl.BlockSpec((tm, HEAD_DIM), lambda b, i: (b * nt + i, 0)),
        ],
        out_specs=pl.BlockSpec((1, N_SLABS, tm, HEAD_DIM), lambda b, i: (b, 0, i, 0)),
        out_shape=jax.ShapeDtypeStruct((batch, N_SLABS, s, HEAD_DIM), BF16),
        compiler_params=_params("parallel", "parallel"),
    )(x, g_norm, w_in, gains, cos, sin)


def _softplus(z):
    return jnp.maximum(z, 0.0) + jnp.log(1.0 + jnp.exp(-jnp.abs(z)))


def _sb_kernel(q_ref, k_ref, v_ref, o_ref, *, blk, nsub):
    i = pl.program_id(2)
    row = lax.broadcasted_iota(jnp.int32, (blk, blk), 0)
    col = lax.broadcasted_iota(jnp.int32, (blk, blk), 1)
    incl = (row >= col).astype(BF16)
    strict = col < row

    def block(q, c, later, acc, diagonal):
        start = pl.multiple_of(jnp.maximum(c, 0) * blk, blk)
        kb = k_ref[0, 0, pl.ds(start, blk), :]
        vb = v_ref[0, 0, pl.ds(start, blk), :]
        z = _dot_nt(q, kb)
        sp = _softplus(z)
        if diagonal:
            sp = jnp.where(strict, sp, 0.0)
        r = _dot(sp.astype(BF16), incl)
        a = jnp.exp(z - r - later)
        total = r[:, :1]
        if diagonal:
            a = jnp.where(strict, a, 0.0)
        else:
            a = jnp.where(c >= 0, a, 0.0)
            total = jnp.where(c >= 0, total, 0.0)
        return later + total, acc + _dot(a.astype(BF16), vb)

    qs = [q_ref[0, 0, j * blk:(j + 1) * blk, :] for j in range(nsub)]
    own = [i * nsub + j for j in range(nsub)]
    state = [block(qs[j], own[j], jnp.zeros((blk, 1), F32), jnp.zeros((blk, HEAD_DIM), F32), True)
             for j in range(nsub)]

    def unfinished(back, laters):
        flags = [jnp.logical_and(own[j] - back >= 0, jnp.min(laters[j]) < SB_ZERO_WEIGHT_LOG)
                 for j in range(nsub)]
        return functools.reduce(jnp.logical_or, flags)

    def cond(carry):
        return carry[1]

    def body(carry):
        back, _, laters, accs = carry
        new = [block(qs[j], own[j] - back, laters[j], accs[j], False) for j in range(nsub)]
        laters, accs = [s[0] for s in new], [s[1] for s in new]
        return back + 1, unfinished(back + 1, laters), laters, accs

    state = [block(qs[j], own[j] - 1, state[j][0], state[j][1], False) for j in range(nsub)]
    laters, accs = [s[0] for s in state], [s[1] for s in state]
    _, _, _, accs = lax.while_loop(cond, body, (2, unfinished(2, laters), laters, accs))
    for j in range(nsub):
        o_ref[0, j * blk:(j + 1) * blk, :] = accs[j].astype(o_ref.dtype)


def stick_breaking_attention(qkv, blk=256, nsub=4):
    batch, _, s, dh = qkv.shape
    nh = N_HEADS_SB
    o = pl.pallas_call(
        functools.partial(_sb_kernel, blk=blk, nsub=nsub),
        grid=(batch, nh, s // (blk * nsub)),
        in_specs=[
            pl.BlockSpec((1, 1, blk * nsub, dh), lambda b, h, i: (b, SLAB_Q_SB + h, i, 0)),
            pl.BlockSpec((1, 1, s, dh), lambda b, h, i: (b, SLAB_K_SB + h, 0, 0)),
            pl.BlockSpec((1, 1, s, dh), lambda b, h, i: (b, SLAB_V_SB + h, 0, 0)),
        ],
        out_specs=pl.BlockSpec((1, blk * nsub, dh), lambda b, h, i: (b, i, h)),
        out_shape=jax.ShapeDtypeStruct((batch, s, nh * dh), BF16),
        compiler_params=_params("parallel", "parallel", "arbitrary"),
    )(qkv, qkv, qkv)
    return o.reshape(batch * s, nh * dh)


DIL_GROUP = 4


def _dil_kernel(q_ref, k_ref, v_ref, o_ref, kf_ref, vf_ref, kg_ref, vg_ref, qf_ref, qg_ref,
                po_ref, pm_ref, pl_ref, of_ref):
    i = pl.program_id(2)
    st = DIL_STEPS
    grp = DIL_GROUP
    span, dh = qf_ref.shape
    seq = kf_ref.shape[0]
    chunk = 512

    @pl.when(i == 0)
    def _():
        def conv(c, carry):
            rows = pl.ds(pl.multiple_of(c * chunk, chunk), chunk)
            kf_ref[rows, :] = k_ref[0, 0, rows, :].astype(F32)
            vf_ref[rows, :] = v_ref[0, 0, rows, :].astype(F32)
            return carry
        lax.fori_loop(0, seq // chunk, conv, 0)

        def regroup(c, carry):
            for r in range(grp):
                src = pl.ds(c * chunk * grp + r, chunk, stride=grp)
                dst = pl.ds(pl.multiple_of(r * (seq // grp) + c * chunk, chunk), chunk)
                kg_ref[dst, :] = kf_ref[src, :]
                vg_ref[dst, :] = vf_ref[src, :]
            return carry
        lax.fori_loop(0, seq // grp // chunk, regroup, 0)

    qf_ref[...] = q_ref[0, 0].astype(F32)
    for r in range(grp):
        qg_ref[r * (span // grp):(r + 1) * (span // grp), :] = (
            qf_ref[pl.ds(r, span // grp, stride=grp), :])

    diff = (lax.broadcasted_iota(jnp.int32, (st, 2 * st), 1)
            - lax.broadcasted_iota(jnp.int32, (st, 2 * st), 0))
    bias_normal = jnp.where((diff >= 0) & (diff <= st), 0.0, NEG_INF)
    bias_first = jnp.where(diff <= 0, 0.0, NEG_INF)
    ones = jnp.ones((2 * st, dh), BF16)

    def strided(start, size, stride):
        return pl.ds(start, size) if stride == 1 else pl.ds(start, size, stride=stride)

    for c, (_, dil) in enumerate(DILATED_CONFIGS):
        g = grp if dil % grp == 0 else 1
        q_buf, k_buf, v_buf = (qg_ref, kg_ref, vg_ref) if g == grp else (qf_ref, kf_ref, vf_ref)

        inner = dil // g
        for u in range(span // st):
            r, blk = u % dil, u // dil
            q_start = (r % g) * (span // g) + blk * st * inner + r // g
            block = i * (span // g) + blk * st * inner
            if blk == 0:
                first = i == 0
                w_offset = jnp.where(first, 0, block - st * inner)
                bias = jnp.where(first, bias_first, bias_normal)
            else:
                w_offset, bias = block - st * inner, bias_normal
            w_start = (r % g) * (seq // g) + w_offset + r // g
            q = q_buf[strided(q_start, st, inner), :].astype(BF16)
            kw = k_buf[strided(w_start, 2 * st, inner), :].astype(BF16)
            vw = v_buf[strided(w_start, 2 * st, inner), :].astype(BF16)
            s = _dot_nt(q, kw) + bias
            m = jnp.max(s, axis=-1, keepdims=True)
            p = jnp.exp(s - m).astype(BF16)
            o_den = _dot(p, jnp.concatenate([vw, ones], axis=1))
            rows = strided(q_start, st, inner)
            po_ref[c, rows, :] = o_den[:, :dh]
            pl_ref[c, rows, :] = o_den[:, dh:]
            pm_ref[c, rows, :] = jnp.broadcast_to(m, (st, dh))

    def merge(c, carry):
        rows_g = pl.ds(pl.multiple_of(c * st, st), st)
        per_group = span // grp // st
        rows_n = pl.ds((c % per_group) * st * grp + c // per_group, st, stride=grp)
        rows = [rows_g if dil % grp == 0 else rows_n for _, dil in DILATED_CONFIGS]
        ms = [pm_ref[b, rows[b], :] for b in range(len(rows))]
        m = functools.reduce(jnp.maximum, ms)
        ws = [jnp.exp(mb - m) for mb in ms]
        den = sum(w * pl_ref[b, rows[b], :] for b, w in enumerate(ws))
        num = sum(w * po_ref[b, rows[b], :] for b, w in enumerate(ws))
        of_ref[rows_n, :] = num / den
        return carry

    lax.fori_loop(0, span // st, merge, 0, unroll=2)
    o_ref[0] = of_ref[...].astype(o_ref.dtype)


def dilated_attention(qkv):
    batch, _, s, dh = qkv.shape
    nh = N_HEADS_DIL
    span = DIL_STEPS * max(dil for _, dil in DILATED_CONFIGS)
    assert s % span == 0 and s >= 2 * span
    n_cfg = len(DILATED_CONFIGS)
    o = pl.pallas_call(
        _dil_kernel,
        grid=(batch, nh, s // span),
        in_specs=[
            pl.BlockSpec((1, 1, span, dh), lambda b, h, i: (b, SLAB_Q_DIL + h, i, 0)),
            pl.BlockSpec((1, 1, s, dh), lambda b, h, i: (b, SLAB_K_DIL + h, 0, 0)),
            pl.BlockSpec((1, 1, s, dh), lambda b, h, i: (b, SLAB_V_DIL + h, 0, 0)),
        ],
        out_specs=pl.BlockSpec((1, span, dh), lambda b, h, i: (b, i, h)),
        out_shape=jax.ShapeDtypeStruct((batch, s, nh * dh), BF16),
        scratch_shapes=(
            [pltpu.VMEM((s, dh), F32)] * 4
            + [pltpu.VMEM((span, dh), F32)] * 2
            + [pltpu.VMEM((n_cfg, span, dh), F32)] * 3
            + [pltpu.VMEM((span, dh), F32)]),
        compiler_params=_params("parallel", "parallel", "arbitrary"),
    )(qkv, qkv, qkv)
    return o.reshape(batch * s, nh * dh)


def _rms(y, g):
    return y * lax.rsqrt(jnp.mean(y * y, axis=-1, keepdims=True) + EPS) * g


def _mix_out(osb_ref, odil_ref, x_ref, gsb_ref, gdil_ref, w_ref, gffn_ref):
    d_sb = osb_ref.shape[1]
    y_sb = _rms(osb_ref[...].astype(F32), gsb_ref[...]).astype(BF16)
    y_dil = _rms(odil_ref[...].astype(F32), gdil_ref[...]).astype(BF16)
    x = x_ref[...] + _dot(y_sb, w_ref[:d_sb, :]) + _dot(y_dil, w_ref[d_sb:, :])
    return x, _rms(x, gffn_ref[...])


def _outproj_router_kernel(osb_ref, odil_ref, x_ref, gsb_ref, gdil_ref, w_ref, gffn_ref, wr_ref,
                           xo_ref, xn_ref, comb_ref, cnt_ref):
    tm = x_ref.shape[0]
    x, xn = _mix_out(osb_ref, odil_ref, x_ref, gsb_ref, gdil_ref, w_ref, gffn_ref)
    xo_ref[...] = x
    xn_hi = xn.astype(BF16)
    xn_ref[...] = xn_hi

    n_e = N_EXPERTS
    xn_lo = (xn - xn_hi.astype(F32)).astype(BF16)
    wr = wr_ref[...]
    wr_hi = wr.astype(BF16).astype(F32)
    lane_w = lax.broadcasted_iota(jnp.int32, wr.shape, 1)
    rhs = jnp.where(lane_w < n_e, wr_hi, wr - wr_hi).astype(BF16)
    prod = _dot(jnp.concatenate([xn_hi, xn_lo], axis=0), rhs)
    hi_rows = prod[:tm]
    logits = hi_rows + pltpu.roll(hi_rows, V7X_LANES - n_e, 1) + prod[tm:]
    lane = lax.broadcasted_iota(jnp.int32, comb_ref.shape, 1)
    logits = jnp.where(lane < n_e, logits, -jnp.inf)
    top1 = jnp.max(logits, axis=-1, keepdims=True)
    idx1 = jnp.min(jnp.where(logits == top1, lane, V7X_LANES), axis=-1, keepdims=True)
    rest = jnp.where(lane == idx1, -jnp.inf, logits)
    top2 = jnp.max(rest, axis=-1, keepdims=True)
    idx2 = jnp.min(jnp.where(rest == top2, lane, V7X_LANES), axis=-1, keepdims=True)
    t = jnp.exp(top2 - top1)
    gate1 = 1.0 / (1.0 + t)
    gate2 = t / (1.0 + t)
    comb = jnp.where(lane == idx1, gate1, 0.0) + jnp.where(lane == idx2, gate2, 0.0)
    comb_ref[...] = comb
    routed = jnp.sum(jnp.where(comb > 0.0, 1.0, 0.0), axis=0, keepdims=True)
    cnt_ref[...] = jnp.broadcast_to(routed, cnt_ref.shape)


def _mix_specs(tm, d, d_sb, d_dil, index):
    row = lambda width: pl.BlockSpec((tm, width), lambda *g: (index(*g), 0))
    const = lambda shape: pl.BlockSpec(shape, lambda *g: (0, 0))
    return [row(d_sb), row(d_dil), row(d), const((1, d_sb)), const((1, d_dil)),
            const((d_sb + d_dil, d)), const((1, d))]


def out_projection_router(o_sb, o_dil, x, g_sb, g_dil, w_out, g_ffn, w_router, tm=512):
    n, d = x.shape
    n_e = w_router.shape[1]
    w_router_2x = jnp.pad(jnp.concatenate([w_router, w_router], axis=1),
                         ((0, 0), (0, V7X_LANES - 2 * n_e)))
    row = lambda width: pl.BlockSpec((tm, width), lambda i: (i, 0))
    return pl.pallas_call(
        _outproj_router_kernel,
        grid=(n // tm,),
        in_specs=_mix_specs(tm, d, o_sb.shape[1], o_dil.shape[1], lambda i: i) + [
            pl.BlockSpec(w_router_2x.shape, lambda i: (0, 0))],
        out_specs=[row(d), row(d), row(V7X_LANES),
                   pl.BlockSpec((V7X_SUBLANES, V7X_LANES), lambda i: (i, 0))],
        out_shape=[jax.ShapeDtypeStruct((n, d), F32), jax.ShapeDtypeStruct((n, d), BF16),
                   jax.ShapeDtypeStruct((n, V7X_LANES), F32),
                   jax.ShapeDtypeStruct((n // tm * V7X_SUBLANES, V7X_LANES), F32)],
        compiler_params=_params("parallel"),
    )(o_sb, o_dil, x, g_sb, g_dil, w_out, g_ffn, w_router_2x)


def _outproj_ffn_kernel(osb_ref, odil_ref, x_ref, gsb_ref, gdil_ref, w_ref, gffn_ref,
                        wg_ref, wu_ref, wd_ref, o_ref, xn_ref):
    @pl.when(pl.program_id(1) == 0)
    def _():
        x, xn = _mix_out(osb_ref, odil_ref, x_ref, gsb_ref, gdil_ref, w_ref, gffn_ref)
        o_ref[...] = x
        xn_ref[...] = xn.astype(BF16)

    xn = xn_ref[...]
    gate = _dot(xn, wg_ref[...])
    up = _dot(xn, wu_ref[...])
    hid = gate * (1.0 / (1.0 + jnp.exp(-gate))) * up
    o_ref[...] += _dot(hid.astype(BF16), wd_ref[...])


def out_projection_ffn(o_sb, o_dil, x, g_sb, g_dil, w_out, g_ffn, wg, wu, wd, tm=512, tf=512):
    n, d = x.shape
    width = wg.shape[1]
    assert width % tf == 0
    return pl.pallas_call(
        _outproj_ffn_kernel,
        grid=(n // tm, width // tf),
        in_specs=_mix_specs(tm, d, o_sb.shape[1], o_dil.shape[1], lambda i, f: i) + [
            pl.BlockSpec((d, tf), lambda i, f: (0, f)),
            pl.BlockSpec((d, tf), lambda i, f: (0, f)),
            pl.BlockSpec((tf, d), lambda i, f: (f, 0))],
        out_specs=pl.BlockSpec((tm, d), lambda i, f: (i, 0)),
        out_shape=jax.ShapeDtypeStruct((n, d), F32),
        scratch_shapes=[pltpu.VMEM((tm, d), BF16)],
        compiler_params=_params("parallel", "arbitrary"),
    )(o_sb, o_dil, x, g_sb, g_dil, w_out, g_ffn, wg, wu, wd)


MOE_ROW_TILE = 512
MOE_SEG_BITS = (32, 16, 8, 4, 2, 1)


def _moe_layout(counts, tm, n_rows):
    n_exp = counts.shape[1]
    n16 = (counts + V7X_BF16_ROWS - 1) // V7X_BF16_ROWS
    rows = n16 * V7X_BF16_ROWS
    filled = jnp.sum(rows, axis=0)
    region = (filled + MOE_ROW_TILE - 1) // MOE_ROW_TILE * MOE_ROW_TILE
    ends = jnp.cumsum(region)
    tile_off = (ends - region)[None, :] + jnp.cumsum(rows, axis=0) - rows
    tail_start = ends - region + filled
    tail_n16 = (region - filled) // V7X_BF16_ROWS
    row_start = jnp.arange(n_rows // MOE_ROW_TILE, dtype=jnp.int32) * MOE_ROW_TILE
    tile_expert = jnp.minimum(jnp.sum(row_start[:, None] >= ends[None, :], axis=1), n_exp - 1)
    n_used = ends[-1:] // MOE_ROW_TILE
    i32 = lambda a: a.astype(jnp.int32)
    return (i32(tile_off).reshape(-1), i32(n16).reshape(-1), i32(tail_start), i32(tail_n16),
            i32(tile_expert), i32(n_used))


def _moe_tile_segments(n16_ref, tile):
    segs, start = [], jnp.int32(0)
    for e in range(N_EXPERTS):
        pieces = n16_ref[tile * N_EXPERTS + e]
        segs.append((start, pieces))
        start = start + pieces * V7X_BF16_ROWS
    return segs


def _moe_tile_slots(comb, segs, n_slots):
    tm = comb.shape[0]
    rows_e = V7X_BF16_ROWS
    comb_t = comb.T[:rows_e, :]
    routed = comb_t > 0.0
    before = (lax.broadcasted_iota(jnp.int32, (tm, tm), 0)
              < lax.broadcasted_iota(jnp.int32, (tm, tm), 1))
    rank = _dot(jnp.where(routed, 1.0, 0.0).astype(BF16),
                jnp.where(before, 1.0, 0.0).astype(BF16))
    sub = lax.broadcasted_iota(jnp.int32, (rows_e, tm), 0)
    seg_col = jnp.zeros((rows_e, tm), F32)
    for e, (start, _) in enumerate(segs):
        seg_col = jnp.where(sub == e, start.astype(F32), seg_col)
    slot = jnp.where(routed, rank + seg_col, -1.0)
    n_routed = jnp.sum(jnp.where(routed, 1.0, 0.0), axis=0, keepdims=True)
    slot_hi = jnp.max(slot, axis=0, keepdims=True)
    slot_lo = jnp.where(n_routed >= 2.0,
                        jnp.sum(jnp.where(routed, slot, 0.0), axis=0, keepdims=True) - slot_hi,
                        -1.0)
    gate_hi = jnp.sum(jnp.where(slot == slot_hi, comb_t, 0.0), axis=0, keepdims=True)
    gate_lo = jnp.sum(comb_t, axis=0, keepdims=True) - gate_hi
    s_idx = lax.broadcasted_iota(jnp.int32, (n_slots, tm), 0).astype(F32)
    hit_hi = s_idx == slot_hi
    hit_lo = s_idx == slot_lo
    onehot = jnp.where(hit_hi, 1.0, jnp.where(hit_lo, 1.0, 0.0)).astype(BF16)
    gate_slot = jnp.sum(jnp.where(hit_hi, gate_hi, jnp.where(hit_lo, gate_lo, 0.0)),
                        axis=1, keepdims=True)
    return onehot, gate_slot


def _piece_copies(pieces, local_rows, hbm_ref, hbm_start, sem, to_hbm):
    copies = []
    for bit in MOE_SEG_BITS:
        done = (pieces // (2 * bit)) * (2 * bit) * V7X_BF16_ROWS
        rows = bit * V7X_BF16_ROWS
        local = local_rows(done, rows)
        remote = hbm_ref.at[pl.ds(pl.multiple_of(hbm_start + done, V7X_BF16_ROWS), rows)]
        src, dst = (local, remote) if to_hbm else (remote, local)
        copies.append(((pieces & bit) != 0, pltpu.make_async_copy(src, dst, sem)))
    return copies


def _segment_copies(buf_ref, hbm_ref, off_ref, n16_ref, tile, sems, to_hbm):
    slot = tile % 2
    copies = []
    for e, (seg_start, pieces) in enumerate(_moe_tile_segments(n16_ref, tile)):
        local_rows = lambda done, rows, seg_start=seg_start: buf_ref.at[
            slot, pl.ds(pl.multiple_of(seg_start + done, V7X_BF16_ROWS), rows)]
        copies += _piece_copies(pieces, local_rows, hbm_ref, off_ref[tile * N_EXPERTS + e],
                                sems.at[slot, e], to_hbm)
    return copies


def _start_all(copies):
    for pred, cp in copies:
        pl.when(pred)(cp.start)


def _wait_all(copies):
    for pred, cp in copies:
        pl.when(pred)(cp.wait)


def _moe_dispatch_kernel(off_ref, n16_ref, tail_start_ref, tail_n16_ref, used_ref, xn_ref,
                         comb_ref, xs_ref, buf_ref, zero_ref, sems, tail_sems):
    tile = pl.program_id(0)
    slot = tile % 2
    d = xn_ref.shape[1]
    n_slots = buf_ref.shape[1]
    segs = _moe_tile_segments(n16_ref, tile)
    onehot, gate_slot = _moe_tile_slots(comb_ref[...], segs, n_slots)
    buf_ref[slot, :, :d] = _dot(onehot, xn_ref[...]).astype(BF16)
    g_hi = gate_slot.astype(BF16).astype(F32)
    g_lo = gate_slot - g_hi
    lane = lax.broadcasted_iota(jnp.int32, (n_slots, V7X_LANES), 1)
    buf_ref[slot, :, d:] = jnp.where(lane == 0, g_hi,
                                     jnp.where(lane == 1, g_lo, 0.0)).astype(BF16)
    copies = _segment_copies(buf_ref, xs_ref, off_ref, n16_ref, tile, sems, to_hbm=True)
    _start_all(copies)

    @pl.when(tile > 0)
    def _():
        _wait_all(_segment_copies(buf_ref, xs_ref, off_ref, n16_ref, tile - 1, sems, to_hbm=True))

    @pl.when(tile == pl.num_programs(0) - 1)
    def _():
        _wait_all(copies)
        zero_ref[...] = jnp.zeros_like(zero_ref)
        tails = []
        for e in range(N_EXPERTS):
            tails += _piece_copies(tail_n16_ref[e], lambda done, rows: zero_ref.at[pl.ds(0, rows)],
                                   xs_ref, tail_start_ref[e], tail_sems.at[e], to_hbm=True)
        _start_all(tails)
        _wait_all(tails)

        def zero_row_tile(r, carry):
            rows = pl.ds(pl.multiple_of(r * MOE_ROW_TILE, MOE_ROW_TILE), MOE_ROW_TILE)
            cp = pltpu.make_async_copy(zero_ref, xs_ref.at[rows], tail_sems.at[0])
            cp.start()
            cp.wait()
            return carry
        lax.fori_loop(used_ref[0], xs_ref.shape[0] // MOE_ROW_TILE, zero_row_tile, 0)


def _moe_expert_kernel(te_ref, used_ref, xs_ref, wg_ref, wu_ref, wd_ref, ys_ref):
    d = wg_ref.shape[1]
    live = pl.program_id(0) < used_ref[0]

    @pl.when(live)
    def _():
        xs = xs_ref[:, :d]
        weight = xs_ref[:, d:d + 1].astype(F32) + xs_ref[:, d + 1:d + 2].astype(F32)
        gate = _dot(xs, wg_ref[0])
        up = _dot(xs, wu_ref[0])
        hid = gate * (1.0 / (1.0 + jnp.exp(-gate))) * up * weight
        ys_ref[...] = _dot(hid.astype(BF16), wd_ref[0]).astype(BF16)

    @pl.when(jnp.logical_not(live))
    def _():
        ys_ref[...] = jnp.zeros_like(ys_ref)


def _moe_combine_kernel(off_ref, n16_ref, ys_ref, comb_ref, x_ref, o_ref, buf_ref, sems):
    tile = pl.program_id(0)
    n_slots = buf_ref.shape[1]
    fetch = lambda t: _segment_copies(buf_ref, ys_ref, off_ref, n16_ref, t, sems, to_hbm=False)

    @pl.when(tile == 0)
    def _():
        buf_ref[...] = jnp.zeros_like(buf_ref)
        _start_all(fetch(tile))

    @pl.when(tile + 1 < pl.num_programs(0))
    def _():
        _start_all(fetch(tile + 1))

    onehot, _ = _moe_tile_slots(comb_ref[...], _moe_tile_segments(n16_ref, tile), n_slots)
    _wait_all(fetch(tile))
    o_ref[...] = x_ref[...] + lax.dot_general(onehot, buf_ref[tile % 2], (((0,), (0,)), ((), ())),
                                              preferred_element_type=F32)


def moe_ffn(xn, x, comb, counts, wg, wu, wd, tm=512):
    n, d = x.shape
    n_exp, _, width = wg.shape
    n_tiles = n // tm
    seg_pad = n_exp * (V7X_BF16_ROWS - 1)
    n_slots = -(-(2 * tm + seg_pad) // V7X_LANES) * V7X_LANES
    n_rows = 2 * n + n_tiles * seg_pad + n_exp * (MOE_ROW_TILE - 1)
    n_rows = -(-n_rows // MOE_ROW_TILE) * MOE_ROW_TILE
    tile_off, n16, tail_start, tail_n16, tile_expert, n_used = _moe_layout(counts, tm, n_rows)
    d_x = d + V7X_LANES
    any_spec = pl.BlockSpec(memory_space=pl.ANY)
    sems = pltpu.SemaphoreType.DMA((2, n_exp))

    xs = pl.pallas_call(
        _moe_dispatch_kernel,
        grid_spec=pltpu.PrefetchScalarGridSpec(
            num_scalar_prefetch=5, grid=(n_tiles,),
            in_specs=[pl.BlockSpec((tm, d), lambda i, *_: (i, 0)),
                      pl.BlockSpec((tm, V7X_LANES), lambda i, *_: (i, 0))],
            out_specs=any_spec,
            scratch_shapes=[pltpu.VMEM((2, n_slots, d_x), BF16),
                            pltpu.VMEM((MOE_ROW_TILE, d_x), BF16), sems,
                            pltpu.SemaphoreType.DMA((n_exp,))]),
        out_shape=jax.ShapeDtypeStruct((n_rows, d_x), BF16),
        compiler_params=_params("arbitrary"),
    )(tile_off, n16, tail_start, tail_n16, n_used, xn, comb)

    once = pl.Buffered(1)
    ys = pl.pallas_call(
        _moe_expert_kernel,
        grid_spec=pltpu.PrefetchScalarGridSpec(
            num_scalar_prefetch=2, grid=(n_rows // MOE_ROW_TILE,),
            in_specs=[pl.BlockSpec((MOE_ROW_TILE, d_x), lambda r, te, used: (r, 0)),
                      pl.BlockSpec((1, d, width), lambda r, te, used: (te[r], 0, 0),
                                   pipeline_mode=once),
                      pl.BlockSpec((1, d, width), lambda r, te, used: (te[r], 0, 0),
                                   pipeline_mode=once),
                      pl.BlockSpec((1, width, d), lambda r, te, used: (te[r], 0, 0),
                                   pipeline_mode=once)],
            out_specs=pl.BlockSpec((MOE_ROW_TILE, d), lambda r, te, used: (r, 0))),
        out_shape=jax.ShapeDtypeStruct((n_rows, d), BF16),
        compiler_params=_params("arbitrary"),
    )(tile_expert, n_used, xs, wg, wu, wd)

    return pl.pallas_call(
        _moe_combine_kernel,
        grid_spec=pltpu.PrefetchScalarGridSpec(
            num_scalar_prefetch=2, grid=(n_tiles,),
            in_specs=[any_spec, pl.BlockSpec((tm, V7X_LANES), lambda i, *_: (i, 0)),
                      pl.BlockSpec((tm, d), lambda i, *_: (i, 0))],
            out_specs=pl.BlockSpec((tm, d), lambda i, *_: (i, 0)),
            scratch_shapes=[pltpu.VMEM((2, n_slots, d), BF16), sems]),
        out_shape=jax.ShapeDtypeStruct((n, d), F32),
        compiler_params=_params("arbitrary"),
    )(tile_off, n16, ys, comb, x)


def _slab_gains(q_sb, k_sb, q_dil, k_dil):
    ones = jnp.ones((4, HEAD_DIM), F32)
    rep = lambda g: jnp.broadcast_to(g[None, :], (4, HEAD_DIM))
    return jnp.concatenate([rep(q_sb), rep(k_sb), ones, rep(q_dil), rep(k_dil), ones], axis=0)


def kernel(x, positions, norm_mix, w_in, q_norm_sb, k_norm_sb, q_norm_dil, k_norm_dil, out_norm_sb, out_norm_dil, w_out, norm_ffn, w_gate_dense, w_up_dense, w_down_dense, w_router, w_gate_moe, w_up_moe, w_down_moe):
    batch, seq, d = x.shape
    n = batch * seq
    depth = w_in.shape[0]
    xf = x.reshape(n, d)
    pos_b = jnp.broadcast_to(positions.reshape(n, 1), (n, HEAD_DIM))
    cos, sin = rope_tables(pos_b)
    for i in range(depth):
        gains = _slab_gains(q_norm_sb[i], k_norm_sb[i], q_norm_dil[i], k_norm_dil[i])
        qkv = in_projection(xf, norm_mix[i][None, :], w_in[i].astype(BF16), gains, cos, sin, batch)
        o_sb = stick_breaking_attention(qkv)
        o_dil = dilated_attention(qkv)
        j = i // 2
        mix = (o_sb, o_dil, xf, out_norm_sb[i][None, :], out_norm_dil[i][None, :],
               w_out[i].astype(BF16), norm_ffn[i][None, :])
        if i % 2 == 1:
            xf, xn, comb, cnt = out_projection_router(*mix, w_router[j])
            counts = cnt[::V7X_SUBLANES, :N_EXPERTS].astype(jnp.int32)
            xf = moe_ffn(xn, xf, comb, counts, w_gate_moe[j].astype(BF16),
                         w_up_moe[j].astype(BF16), w_down_moe[j].astype(BF16))
        else:
            xf = out_projection_ffn(*mix, w_gate_dense[j].astype(BF16),
                                    w_up_dense[j].astype(BF16), w_down_dense[j].astype(BF16))
    return xf.reshape(batch, seq, d)
```

```python
import functools
import math

import jax
import jax.numpy as jnp
from jax import lax
from jax.experimental import pallas as pl
from jax.experimental.pallas import tpu as pltpu

HEAD_DIM = 128
N_HEADS_SB = 4
N_HEADS_DIL = 4
N_SLABS = 3 * (N_HEADS_SB + N_HEADS_DIL)
SLAB_Q_SB, SLAB_K_SB, SLAB_V_SB = 0, 4, 8
SLAB_Q_DIL, SLAB_K_DIL, SLAB_V_DIL = 12, 16, 20
DILATED_CONFIGS = ((128, 1), (512, 4), (2048, 16))
DIL_STEPS = 128
N_EXPERTS = 8
ROPE_THETA = 10000.0
EPS = 1e-6
NEG_INF = -1e30
SB_ZERO_WEIGHT_LOG = 112.0

V7X_LANES = 128
V7X_SUBLANES = 8
V7X_BF16_ROWS = 16
V7X_VMEM_LIMIT_BYTES = 56 * 1024 * 1024

F32 = jnp.float32
BF16 = jnp.bfloat16


def _params(*semantics):
    return pltpu.CompilerParams(dimension_semantics=semantics,
                                vmem_limit_bytes=V7X_VMEM_LIMIT_BYTES)


def _dot(a, b):
    return jnp.dot(a, b, preferred_element_type=F32)


def _dot_nt(a, b):
    return lax.dot_general(a, b, (((1,), (1,)), ((), ())), preferred_element_type=F32)


def _cast_kernel(w_ref, o_ref):
    o_ref[...] = w_ref[0].astype(o_ref.dtype)


def layer_weights_bf16(w, layer, rows=512):
    flat = w.reshape(w.shape[0], -1, w.shape[-1])
    _, r, c = flat.shape
    rows = min(rows, r)
    assert r % rows == 0
    out = pl.pallas_call(
        _cast_kernel,
        grid=(r // rows,),
        in_specs=[pl.BlockSpec((1, rows, c), lambda k: (layer, k, 0))],
        out_specs=pl.BlockSpec((rows, c), lambda k: (k, 0)),
        out_shape=jax.ShapeDtypeStruct((r, c), BF16),
        compiler_params=_params("parallel"),
    )(flat)
    return out.reshape(w.shape[1:])


def _rope_kernel(pos_ref, cos_ref, sin_ref):
    lane = lax.broadcasted_iota(jnp.int32, pos_ref.shape, 1)
    half = HEAD_DIM // 2
    pair = (lane % half).astype(F32)
    inv = jnp.exp(pair * (-2.0 * math.log(ROPE_THETA) / HEAD_DIM))
    ang = pos_ref[...].astype(F32) * inv
    cos_ref[...] = jnp.cos(ang)
    s = jnp.sin(ang)
    sin_ref[...] = jnp.where(lane < half, -s, s)


def rope_tables(pos_b, tm=1024):
    n = pos_b.shape[0]
    spec = pl.BlockSpec((tm, HEAD_DIM), lambda i: (i, 0))
    return pl.pallas_call(
        _rope_kernel,
        grid=(n // tm,),
        in_specs=[spec],
        out_specs=[spec, spec],
        out_shape=[jax.ShapeDtypeStruct((n, HEAD_DIM), F32)] * 2,
        compiler_params=_params("parallel"),
    )(pos_b)


def _inproj_kernel(x_ref, g_ref, w_ref, gains_ref, cos_ref, sin_ref, o_ref):
    scale = 1.0 / math.sqrt(HEAD_DIM)
    group = 4
    tm = x_ref.shape[0]
    for rows in (slice(0, tm // 2), slice(tm // 2, tm)):
        x = x_ref[rows, :]
        ms = jnp.mean(x * x, axis=-1, keepdims=True)
        xn = (x * lax.rsqrt(ms + EPS) * g_ref[...]).astype(BF16)
        cos = cos_ref[rows, :]
        sin = sin_ref[rows, :]
        for j0 in range(0, N_SLABS, group):
            h = _dot(xn, w_ref[:, j0 * HEAD_DIM:(j0 + group) * HEAD_DIM])
            for j in range(j0, j0 + group):
                y = h[:, (j - j0) * HEAD_DIM:(j - j0 + 1) * HEAD_DIM]
                is_v = SLAB_V_SB <= j < SLAB_Q_DIL or j >= SLAB_V_DIL
                if not is_v:
                    ms_h = jnp.mean(y * y, axis=-1, keepdims=True)
                    y = y * lax.rsqrt(ms_h + EPS) * gains_ref[j:j + 1, :]
                    if j >= SLAB_Q_DIL:
                        y = y * cos + pltpu.roll(y, HEAD_DIM // 2, 1) * sin
                    if j < SLAB_K_SB or SLAB_Q_DIL <= j < SLAB_K_DIL:
                        y = y * scale
                o_ref[0, j, rows, :] = y.astype(BF16)


def in_projection(x, g_norm, w_in, gains, cos, sin, batch, tm=512):
    n, d = x.shape
    s = n // batch
    d_in = w_in.shape[1]
    nt = s // tm
    return pl.pallas_call(
        _inproj_kernel,
        grid=(batch, nt),
        in_specs=[
            pl.BlockSpec((tm, d), lambda b, i: (b * nt + i, 0)),
            pl.BlockSpec((1, d), lambda b, i: (0, 0)),
            pl.BlockSpec((d, d_in), lambda b, i: (0, 0)),
            pl.BlockSpec((N_SLABS, HEAD_DIM), lambda b, i: (0, 0)),
            pl.BlockSpec((tm, HEAD_DIM), lambda b, i: (b * nt + i, 0)),
            pl.BlockSpec((tm, HEAD_DIM), lambda b, i: (b * nt + i, 0)),
        ],
        out_specs=pl.BlockSpec((1, N_SLABS, tm, HEAD_DIM), lambda b, i: (b, 0, i, 0)),
        out_shape=jax.ShapeDtypeStruct((batch, N_SLABS, s, HEAD_DIM), BF16),
        compiler_params=_params("parallel", "parallel"),
    )(x, g_norm, w_in, gains, cos, sin)


def _softplus(z):
    return jnp.maximum(z, 0.0) + jnp.log(1.0 + jnp.exp(-jnp.abs(z)))


def _sb_kernel(q_ref, k_ref, v_ref, o_ref, *, blk, nsub):
    i = pl.program_id(2)
    row = lax.broadcasted_iota(jnp.int32, (blk, blk), 0)
    col = lax.broadcasted_iota(jnp.int32, (blk, blk), 1)
    incl = (row >= col).astype(BF16)
    strict = col < row

    def block(q, c, later, acc, diagonal):
        start = pl.multiple_of(jnp.maximum(c, 0) * blk, blk)
        kb = k_ref[0, 0, pl.ds(start, blk), :]
        vb = v_ref[0, 0, pl.ds(start, blk), :]
        z = _dot_nt(q, kb)
        sp = _softplus(z)
        if diagonal:
            sp = jnp.where(strict, sp, 0.0)
        r = _dot(sp.astype(BF16), incl)
        a = jnp.exp(z - r - later)
        total = r[:, :1]
        if diagonal:
            a = jnp.where(strict, a, 0.0)
        else:
            a = jnp.where(c >= 0, a, 0.0)
            total = jnp.where(c >= 0, total, 0.0)
        return later + total, acc + _dot(a.astype(BF16), vb)

    qs = [q_ref[0, 0, j * blk:(j + 1) * blk, :] for j in range(nsub)]
    own = [i * nsub + j for j in range(nsub)]
    state = [block(qs[j], own[j], jnp.zeros((blk, 1), F32), jnp.zeros((blk, HEAD_DIM), F32), True)
             for j in range(nsub)]

    def unfinished(back, laters):
        flags = [jnp.logical_and(own[j] - back >= 0, jnp.min(laters[j]) < SB_ZERO_WEIGHT_LOG)
                 for j in range(nsub)]
        return functools.reduce(jnp.logical_or, flags)

    def cond(carry):
        return carry[1]

    def body(carry):
        back, _, laters, accs = carry
        new = [block(qs[j], own[j] - back, laters[j], accs[j], False) for j in range(nsub)]
        laters, accs = [s[0] for s in new], [s[1] for s in new]
        return back + 1, unfinished(back + 1, laters), laters, accs

    state = [block(qs[j], own[j] - 1, state[j][0], state[j][1], False) for j in range(nsub)]
    laters, accs = [s[0] for s in state], [s[1] for s in state]
    _, _, _, accs = lax.while_loop(cond, body, (2, unfinished(2, laters), laters, accs))
    for j in range(nsub):
        o_ref[0, j * blk:(j + 1) * blk, :] = accs[j].astype(o_ref.dtype)


def stick_breaking_attention(qkv, blk=256, nsub=4):
    batch, _, s, dh = qkv.shape
    nh = N_HEADS_SB
    o = pl.pallas_call(
        functools.partial(_sb_kernel, blk=blk, nsub=nsub),
        grid=(batch, nh, s // (blk * nsub)),
        in_specs=[
            pl.BlockSpec((1, 1, blk * nsub, dh), lambda b, h, i: (b, SLAB_Q_SB + h, i, 0)),
            pl.BlockSpec((1, 1, s, dh), lambda b, h, i: (b, SLAB_K_SB + h, 0, 0)),
            pl.BlockSpec((1, 1, s, dh), lambda b, h, i: (b, SLAB_V_SB + h, 0, 0)),
        ],
        out_specs=pl.BlockSpec((1, blk * nsub, dh), lambda b, h, i: (b, i, h)),
        out_shape=jax.ShapeDtypeStruct((batch, s, nh * dh), BF16),
        compiler_params=_params("parallel", "parallel", "arbitrary"),
    )(qkv, qkv, qkv)
    return o.reshape(batch * s, nh * dh)


DIL_GROUP = 4


def _dil_kernel(q_ref, k_ref, v_ref, o_ref, kf_ref, vf_ref, kg_ref, vg_ref, qf_ref, qg_ref,
                po_ref, pm_ref, pl_ref, of_ref):
    i = pl.program_id(2)
    st = DIL_STEPS
    grp = DIL_GROUP
    span, dh = qf_ref.shape
    seq = kf_ref.shape[0]
    chunk = 512

    @pl.when(i == 0)
    def _():
        def conv(c, carry):
            rows = pl.ds(pl.multiple_of(c * chunk, chunk), chunk)
            kf_ref[rows, :] = k_ref[0, 0, rows, :].astype(F32)
            vf_ref[rows, :] = v_ref[0, 0, rows, :].astype(F32)
            return carry
        lax.fori_loop(0, seq // chunk, conv, 0)

        def regroup(c, carry):
            for r in range(grp):
                src = pl.ds(c * chunk * grp + r, chunk, stride=grp)
                dst = pl.ds(pl.multiple_of(r * (seq // grp) + c * chunk, chunk), chunk)
                kg_ref[dst, :] = kf_ref[src, :]
                vg_ref[dst, :] = vf_ref[src, :]
            return carry
        lax.fori_loop(0, seq // grp // chunk, regroup, 0)

    qf_ref[...] = q_ref[0, 0].astype(F32)
    for r in range(grp):
        qg_ref[r * (span // grp):(r + 1) * (span // grp), :] = (
            qf_ref[pl.ds(r, span // grp, stride=grp), :])

    diff = (lax.broadcasted_iota(jnp.int32, (st, 2 * st), 1)
            - lax.broadcasted_iota(jnp.int32, (st, 2 * st), 0))
    bias_normal = jnp.where((diff >= 0) & (diff <= st), 0.0, NEG_INF)
    bias_first = jnp.where(diff <= 0, 0.0, NEG_INF)
    ones = jnp.ones((2 * st, dh), BF16)

    def strided(start, size, stride):
        return pl.ds(start, size) if stride == 1 else pl.ds(start, size, stride=stride)

    for c, (_, dil) in enumerate(DILATED_CONFIGS):
        g = grp if dil % grp == 0 else 1
        q_buf, k_buf, v_buf = (qg_ref, kg_ref, vg_ref) if g == grp else (qf_ref, kf_ref, vf_ref)

        inner = dil // g
        for u in range(span // st):
            r, blk = u % dil, u // dil
            q_start = (r % g) * (span // g) + blk * st * inner + r // g
            block = i * (span // g) + blk * st * inner
            if blk == 0:
                first = i == 0
                w_offset = jnp.where(first, 0, block - st * inner)
                bias = jnp.where(first, bias_first, bias_normal)
            else:
                w_offset, bias = block - st * inner, bias_normal
            w_start = (r % g) * (seq // g) + w_offset + r // g
            q = q_buf[strided(q_start, st, inner), :].astype(BF16)
            kw = k_buf[strided(w_start, 2 * st, inner), :].astype(BF16)
            vw = v_buf[strided(w_start, 2 * st, inner), :].astype(BF16)
            s = _dot_nt(q, kw) + bias
            m = jnp.max(s, axis=-1, keepdims=True)
            p = jnp.exp(s - m).astype(BF16)
            o_den = _dot(p, jnp.concatenate([vw, ones], axis=1))
            rows = strided(q_start, st, inner)
            po_ref[c, rows, :] = o_den[:, :dh]
            pl_ref[c, rows, :] = o_den[:, dh:]
            pm_ref[c, rows, :] = jnp.broadcast_to(m, (st, dh))

    def merge(c, carry):
        rows_g = pl.ds(pl.multiple_of(c * st, st), st)
        per_group = span // grp // st
        rows_n = pl.ds((c % per_group) * st * grp + c // per_group, st, stride=grp)
        rows = [rows_g if dil % grp == 0 else rows_n for _, dil in DILATED_CONFIGS]
        ms = [pm_ref[b, rows[b], :] for b in range(len(rows))]
        m = functools.reduce(jnp.maximum, ms)
        ws = [jnp.exp(mb - m) for mb in ms]
        den = sum(w * pl_ref[b, rows[b], :] for b, w in enumerate(ws))
        num = sum(w * po_ref[b, rows[b], :] for b, w in enumerate(ws))
        of_ref[rows_n, :] = num / den
        return carry

    lax.fori_loop(0, span // st, merge, 0, unroll=2)
    o_ref[0] = of_ref[...].astype(o_ref.dtype)


def dilated_attention(qkv):
    batch, _, s, dh = qkv.shape
    nh = N_HEADS_DIL
    span = DIL_STEPS * max(dil for _, dil in DILATED_CONFIGS)
    assert s % span == 0 and s >= 2 * span
    n_cfg = len(DILATED_CONFIGS)
    o = pl.pallas_call(
        _dil_kernel,
        grid=(batch, nh, s // span),
        in_specs=[
            pl.BlockSpec((1, 1, span, dh), lambda b, h, i: (b, SLAB_Q_DIL + h, i, 0)),
            pl.BlockSpec((1, 1, s, dh), lambda b, h, i: (b, SLAB_K_DIL + h, 0, 0)),
            pl.BlockSpec((1, 1, s, dh), lambda b, h, i: (b, SLAB_V_DIL + h, 0, 0)),
        ],
        out_specs=pl.BlockSpec((1, span, dh), lambda b, h, i: (b, i, h)),
        out_shape=jax.ShapeDtypeStruct((batch, s, nh * dh), BF16),
        scratch_shapes=(
            [pltpu.VMEM((s, dh), F32)] * 4
            + [pltpu.VMEM((span, dh), F32)] * 2
            + [pltpu.VMEM((n_cfg, span, dh), F32)] * 3
            + [pltpu.VMEM((span, dh), F32)]),
        compiler_params=_params("parallel", "parallel", "arbitrary"),
    )(qkv, qkv, qkv)
    return o.reshape(batch * s, nh * dh)


def _rms(y, g):
    return y * lax.rsqrt(jnp.mean(y * y, axis=-1, keepdims=True) + EPS) * g


def _mix_out(osb_ref, odil_ref, x_ref, gsb_ref, gdil_ref, w_ref, gffn_ref):
    d_sb = osb_ref.shape[1]
    y_sb = _rms(osb_ref[...].astype(F32), gsb_ref[...]).astype(BF16)
    y_dil = _rms(odil_ref[...].astype(F32), gdil_ref[...]).astype(BF16)
    x = x_ref[...] + _dot(y_sb, w_ref[:d_sb, :]) + _dot(y_dil, w_ref[d_sb:, :])
    return x, _rms(x, gffn_ref[...])


def _outproj_router_kernel(osb_ref, odil_ref, x_ref, gsb_ref, gdil_ref, w_ref, gffn_ref, wr_ref,
                           xo_ref, xn_ref, comb_ref, cnt_ref):
    tm = x_ref.shape[0]
    x, xn = _mix_out(osb_ref, odil_ref, x_ref, gsb_ref, gdil_ref, w_ref, gffn_ref)
    xo_ref[...] = x
    xn_hi = xn.astype(BF16)
    xn_ref[...] = xn_hi

    n_e = N_EXPERTS
    xn_lo = (xn - xn_hi.astype(F32)).astype(BF16)
    wr = wr_ref[...]
    wr_hi = wr.astype(BF16).astype(F32)
    lane_w = lax.broadcasted_iota(jnp.int32, wr.shape, 1)
    rhs = jnp.where(lane_w < n_e, wr_hi, wr - wr_hi).astype(BF16)
    prod = _dot(jnp.concatenate([xn_hi, xn_lo], axis=0), rhs)
    hi_rows = prod[:tm]
    logits = hi_rows + pltpu.roll(hi_rows, V7X_LANES - n_e, 1) + prod[tm:]
    lane = lax.broadcasted_iota(jnp.int32, comb_ref.shape, 1)
    logits = jnp.where(lane < n_e, logits, -jnp.inf)
    top1 = jnp.max(logits, axis=-1, keepdims=True)
    idx1 = jnp.min(jnp.where(logits == top1, lane, V7X_LANES), axis=-1, keepdims=True)
    rest = jnp.where(lane == idx1, -jnp.inf, logits)
    top2 = jnp.max(rest, axis=-1, keepdims=True)
    idx2 = jnp.min(jnp.where(rest == top2, lane, V7X_LANES), axis=-1, keepdims=True)
    t = jnp.exp(top2 - top1)
    gate1 = 1.0 / (1.0 + t)
    gate2 = t / (1.0 + t)
    comb = jnp.where(lane == idx1, gate1, 0.0) + jnp.where(lane == idx2, gate2, 0.0)
    comb_ref[...] = comb
    routed = jnp.sum(jnp.where(comb > 0.0, 1.0, 0.0), axis=0, keepdims=True)
    cnt_ref[...] = jnp.broadcast_to(routed, cnt_ref.shape)


def _mix_specs(tm, d, d_sb, d_dil, index):
    row = lambda width: pl.BlockSpec((tm, width), lambda *g: (index(*g), 0))
    const = lambda shape: pl.BlockSpec(shape, lambda *g: (0, 0))
    return [row(d_sb), row(d_dil), row(d), const((1, d_sb)), const((1, d_dil)),
            const((d_sb + d_dil, d)), const((1, d))]


def out_projection_router(o_sb, o_dil, x, g_sb, g_dil, w_out, g_ffn, w_router, tm=512):
    n, d = x.shape
    n_e = w_router.shape[1]
    w_router_2x = jnp.pad(jnp.concatenate([w_router, w_router], axis=1),
                         ((0, 0), (0, V7X_LANES - 2 * n_e)))
    row = lambda width: pl.BlockSpec((tm, width), lambda i: (i, 0))
    return pl.pallas_call(
        _outproj_router_kernel,
        grid=(n // tm,),
        in_specs=_mix_specs(tm, d, o_sb.shape[1], o_dil.shape[1], lambda i: i) + [
            pl.BlockSpec(w_router_2x.shape, lambda i: (0, 0))],
        out_specs=[row(d), row(d), row(V7X_LANES),
                   pl.BlockSpec((V7X_SUBLANES, V7X_LANES), lambda i: (i, 0))],
        out_shape=[jax.ShapeDtypeStruct((n, d), F32), jax.ShapeDtypeStruct((n, d), BF16),
                   jax.ShapeDtypeStruct((n, V7X_LANES), F32),
                   jax.ShapeDtypeStruct((n // tm * V7X_SUBLANES, V7X_LANES), F32)],
        compiler_params=_params("parallel"),
    )(o_sb, o_dil, x, g_sb, g_dil, w_out, g_ffn, w_router_2x)


def _outproj_ffn_kernel(osb_ref, odil_ref, x_ref, gsb_ref, gdil_ref, w_ref, gffn_ref,
                        wg_ref, wu_ref, wd_ref, o_ref, xn_ref):
    @pl.when(pl.program_id(1) == 0)
    def _():
        x, xn = _mix_out(osb_ref, odil_ref, x_ref, gsb_ref, gdil_ref, w_ref, gffn_ref)
        o_ref[...] = x
        xn_ref[...] = xn.astype(BF16)

    xn = xn_ref[...]
    gate = _dot(xn, wg_ref[...])
    up = _dot(xn, wu_ref[...])
    hid = gate * (1.0 / (1.0 + jnp.exp(-gate))) * up
    o_ref[...] += _dot(hid.astype(BF16), wd_ref[...])


def out_projection_ffn(o_sb, o_dil, x, g_sb, g_dil, w_out, g_ffn, wg, wu, wd, tm=512, tf=512):
    n, d = x.shape
    width = wg.shape[1]
    assert width % tf == 0
    return pl.pallas_call(
        _outproj_ffn_kernel,
        grid=(n // tm, width // tf),
        in_specs=_mix_specs(tm, d, o_sb.shape[1], o_dil.shape[1], lambda i, f: i) + [
            pl.BlockSpec((d, tf), lambda i, f: (0, f)),
            pl.BlockSpec((d, tf), lambda i, f: (0, f)),
            pl.BlockSpec((tf, d), lambda i, f: (f, 0))],
        out_specs=pl.BlockSpec((tm, d), lambda i, f: (i, 0)),
        out_shape=jax.ShapeDtypeStruct((n, d), F32),
        scratch_shapes=[pltpu.VMEM((tm, d), BF16)],
        compiler_params=_params("parallel", "arbitrary"),
    )(o_sb, o_dil, x, g_sb, g_dil, w_out, g_ffn, wg, wu, wd)


MOE_ROW_TILE = 512
MOE_SEG_BITS = (32, 16, 8, 4, 2, 1)


def _moe_layout(counts, tm, n_rows):
    n_exp = counts.shape[1]
    n16 = (counts + V7X_BF16_ROWS - 1) // V7X_BF16_ROWS
    rows = n16 * V7X_BF16_ROWS
    filled = jnp.sum(rows, axis=0)
    region = (filled + MOE_ROW_TILE - 1) // MOE_ROW_TILE * MOE_ROW_TILE
    ends = jnp.cumsum(region)
    tile_off = (ends - region)[None, :] + jnp.cumsum(rows, axis=0) - rows
    tail_start = ends - region + filled
    tail_n16 = (region - filled) // V7X_BF16_ROWS
    row_start = jnp.arange(n_rows // MOE_ROW_TILE, dtype=jnp.int32) * MOE_ROW_TILE
    tile_expert = jnp.minimum(jnp.sum(row_start[:, None] >= ends[None, :], axis=1), n_exp - 1)
    n_used = ends[-1:] // MOE_ROW_TILE
    i32 = lambda a: a.astype(jnp.int32)
    return (i32(tile_off).reshape(-1), i32(n16).reshape(-1), i32(tail_start), i32(tail_n16),
            i32(tile_expert), i32(n_used))


def _moe_tile_segments(n16_ref, tile):
    segs, start = [], jnp.int32(0)
    for e in range(N_EXPERTS):
        pieces = n16_ref[tile * N_EXPERTS + e]
        segs.append((start, pieces))
        start = start + pieces * V7X_BF16_ROWS
    return segs


def _moe_tile_slots(comb, segs, n_slots):
    tm = comb.shape[0]
    rows_e = V7X_BF16_ROWS
    comb_t = comb.T[:rows_e, :]
    routed = comb_t > 0.0
    before = (lax.broadcasted_iota(jnp.int32, (tm, tm), 0)
              < lax.broadcasted_iota(jnp.int32, (tm, tm), 1))
    rank = _dot(jnp.where(routed, 1.0, 0.0).astype(BF16),
                jnp.where(before, 1.0, 0.0).astype(BF16))
    sub = lax.broadcasted_iota(jnp.int32, (rows_e, tm), 0)
    seg_col = jnp.zeros((rows_e, tm), F32)
    for e, (start, _) in enumerate(segs):
        seg_col = jnp.where(sub == e, start.astype(F32), seg_col)
    slot = jnp.where(routed, rank + seg_col, -1.0)
    n_routed = jnp.sum(jnp.where(routed, 1.0, 0.0), axis=0, keepdims=True)
    slot_hi = jnp.max(slot, axis=0, keepdims=True)
    slot_lo = jnp.where(n_routed >= 2.0,
                        jnp.sum(jnp.where(routed, slot, 0.0), axis=0, keepdims=True) - slot_hi,
                        -1.0)
    gate_hi = jnp.sum(jnp.where(slot == slot_hi, comb_t, 0.0), axis=0, keepdims=True)
    gate_lo = jnp.sum(comb_t, axis=0, keepdims=True) - gate_hi
    s_idx = lax.broadcasted_iota(jnp.int32, (n_slots, tm), 0).astype(F32)
    hit_hi = s_idx == slot_hi
    hit_lo = s_idx == slot_lo
    onehot = jnp.where(hit_hi, 1.0, jnp.where(hit_lo, 1.0, 0.0)).astype(BF16)
    gate_slot = jnp.sum(jnp.where(hit_hi, gate_hi, jnp.where(hit_lo, gate_lo, 0.0)),
                        axis=1, keepdims=True)
    return onehot, gate_slot


def _piece_copies(pieces, local_rows, hbm_ref, hbm_start, sem, to_hbm):
    copies = []
    for bit in MOE_SEG_BITS:
        done = (pieces // (2 * bit)) * (2 * bit) * V7X_BF16_ROWS
        rows = bit * V7X_BF16_ROWS
        local = local_rows(done, rows)
        remote = hbm_ref.at[pl.ds(pl.multiple_of(hbm_start + done, V7X_BF16_ROWS), rows)]
        src, dst = (local, remote) if to_hbm else (remote, local)
        copies.append(((pieces & bit) != 0, pltpu.make_async_copy(src, dst, sem)))
    return copies


def _segment_copies(buf_ref, hbm_ref, off_ref, n16_ref, tile, sems, to_hbm):
    slot = tile % 2
    copies = []
    for e, (seg_start, pieces) in enumerate(_moe_tile_segments(n16_ref, tile)):
        local_rows = lambda done, rows, seg_start=seg_start: buf_ref.at[
            slot, pl.ds(pl.multiple_of(seg_start + done, V7X_BF16_ROWS), rows)]
        copies += _piece_copies(pieces, local_rows, hbm_ref, off_ref[tile * N_EXPERTS + e],
                                sems.at[slot, e], to_hbm)
    return copies


def _start_all(copies):
    for pred, cp in copies:
        pl.when(pred)(cp.start)


def _wait_all(copies):
    for pred, cp in copies:
        pl.when(pred)(cp.wait)


def _moe_dispatch_kernel(off_ref, n16_ref, tail_start_ref, tail_n16_ref, used_ref, xn_ref,
                         comb_ref, xs_ref, buf_ref, zero_ref, sems, tail_sems):
    tile = pl.program_id(0)
    slot = tile % 2
    d = xn_ref.shape[1]
    n_slots = buf_ref.shape[1]
    segs = _moe_tile_segments(n16_ref, tile)
    onehot, gate_slot = _moe_tile_slots(comb_ref[...], segs, n_slots)
    buf_ref[slot, :, :d] = _dot(onehot, xn_ref[...]).astype(BF16)
    g_hi = gate_slot.astype(BF16).astype(F32)
    g_lo = gate_slot - g_hi
    lane = lax.broadcasted_iota(jnp.int32, (n_slots, V7X_LANES), 1)
    buf_ref[slot, :, d:] = jnp.where(lane == 0, g_hi,
                                     jnp.where(lane == 1, g_lo, 0.0)).astype(BF16)
    copies = _segment_copies(buf_ref, xs_ref, off_ref, n16_ref, tile, sems, to_hbm=True)
    _start_all(copies)

    @pl.when(tile > 0)
    def _():
        _wait_all(_segment_copies(buf_ref, xs_ref, off_ref, n16_ref, tile - 1, sems, to_hbm=True))

    @pl.when(tile == pl.num_programs(0) - 1)
    def _():
        _wait_all(copies)
        zero_ref[...] = jnp.zeros_like(zero_ref)
        tails = []
        for e in range(N_EXPERTS):
            tails += _piece_copies(tail_n16_ref[e], lambda done, rows: zero_ref.at[pl.ds(0, rows)],
                                   xs_ref, tail_start_ref[e], tail_sems.at[e], to_hbm=True)
        _start_all(tails)
        _wait_all(tails)

        def zero_row_tile(r, carry):
            rows = pl.ds(pl.multiple_of(r * MOE_ROW_TILE, MOE_ROW_TILE), MOE_ROW_TILE)
            cp = pltpu.make_async_copy(zero_ref, xs_ref.at[rows], tail_sems.at[0])
            cp.start()
            cp.wait()
            return carry
        lax.fori_loop(used_ref[0], xs_ref.shape[0] // MOE_ROW_TILE, zero_row_tile, 0)


def _moe_expert_kernel(te_ref, used_ref, xs_ref, wg_ref, wu_ref, wd_ref, ys_ref):
    d = wg_ref.shape[1]
    live = pl.program_id(0) < used_ref[0]

    @pl.when(live)
    def _():
        xs = xs_ref[:, :d]
        weight = xs_ref[:, d:d + 1].astype(F32) + xs_ref[:, d + 1:d + 2].astype(F32)
        gate = _dot(xs, wg_ref[0])
        up = _dot(xs, wu_ref[0])
        hid = gate * (1.0 / (1.0 + jnp.exp(-gate))) * up * weight
        ys_ref[...] = _dot(hid.astype(BF16), wd_ref[0]).astype(BF16)

    @pl.when(jnp.logical_not(live))
    def _():
        ys_ref[...] = jnp.zeros_like(ys_ref)


def _moe_combine_kernel(off_ref, n16_ref, ys_ref, comb_ref, x_ref, o_ref, buf_ref, sems):
    tile = pl.program_id(0)
    n_slots = buf_ref.shape[1]
    fetch = lambda t: _segment_copies(buf_ref, ys_ref, off_ref, n16_ref, t, sems, to_hbm=False)

    @pl.when(tile == 0)
    def _():
        buf_ref[...] = jnp.zeros_like(buf_ref)
        _start_all(fetch(tile))

    @pl.when(tile + 1 < pl.num_programs(0))
    def _():
        _start_all(fetch(tile + 1))

    onehot, _ = _moe_tile_slots(comb_ref[...], _moe_tile_segments(n16_ref, tile), n_slots)
    _wait_all(fetch(tile))
    o_ref[...] = x_ref[...] + lax.dot_general(onehot, buf_ref[tile % 2], (((0,), (0,)), ((), ())),
                                              preferred_element_type=F32)


def moe_ffn(xn, x, comb, counts, wg, wu, wd, tm=512):
    n, d = x.shape
    n_exp, _, width = wg.shape
    n_tiles = n // tm
    seg_pad = n_exp * (V7X_BF16_ROWS - 1)
    n_slots = -(-(2 * tm + seg_pad) // V7X_LANES) * V7X_LANES
    n_rows = 2 * n + n_tiles * seg_pad + n_exp * (MOE_ROW_TILE - 1)
    n_rows = -(-n_rows // MOE_ROW_TILE) * MOE_ROW_TILE
    tile_off, n16, tail_start, tail_n16, tile_expert, n_used = _moe_layout(counts, tm, n_rows)
    d_x = d + V7X_LANES
    any_spec = pl.BlockSpec(memory_space=pl.ANY)
    sems = pltpu.SemaphoreType.DMA((2, n_exp))

    xs = pl.pallas_call(
        _moe_dispatch_kernel,
        grid_spec=pltpu.PrefetchScalarGridSpec(
            num_scalar_prefetch=5, grid=(n_tiles,),
            in_specs=[pl.BlockSpec((tm, d), lambda i, *_: (i, 0)),
                      pl.BlockSpec((tm, V7X_LANES), lambda i, *_: (i, 0))],
            out_specs=any_spec,
            scratch_shapes=[pltpu.VMEM((2, n_slots, d_x), BF16),
                            pltpu.VMEM((MOE_ROW_TILE, d_x), BF16), sems,
                            pltpu.SemaphoreType.DMA((n_exp,))]),
        out_shape=jax.ShapeDtypeStruct((n_rows, d_x), BF16),
        compiler_params=_params("arbitrary"),
    )(tile_off, n16, tail_start, tail_n16, n_used, xn, comb)

    once = pl.Buffered(1)
    ys = pl.pallas_call(
        _moe_expert_kernel,
        grid_spec=pltpu.PrefetchScalarGridSpec(
            num_scalar_prefetch=2, grid=(n_rows // MOE_ROW_TILE,),
            in_specs=[pl.BlockSpec((MOE_ROW_TILE, d_x), lambda r, te, used: (r, 0)),
                      pl.BlockSpec((1, d, width), lambda r, te, used: (te[r], 0, 0),
                                   pipeline_mode=once),
                      pl.BlockSpec((1, d, width), lambda r, te, used: (te[r], 0, 0),
                                   pipeline_mode=once),
                      pl.BlockSpec((1, width, d), lambda r, te, used: (te[r], 0, 0),
                                   pipeline_mode=once)],
            out_specs=pl.BlockSpec((MOE_ROW_TILE, d), lambda r, te, used: (r, 0))),
        out_shape=jax.ShapeDtypeStruct((n_rows, d), BF16),
        compiler_params=_params("arbitrary"),
    )(tile_expert, n_used, xs, wg, wu, wd)

    return pl.pallas_call(
        _moe_combine_kernel,
        grid_spec=pltpu.PrefetchScalarGridSpec(
            num_scalar_prefetch=2, grid=(n_tiles,),
            in_specs=[any_spec, pl.BlockSpec((tm, V7X_LANES), lambda i, *_: (i, 0)),
                      pl.BlockSpec((tm, d), lambda i, *_: (i, 0))],
            out_specs=pl.BlockSpec((tm, d), lambda i, *_: (i, 0)),
            scratch_shapes=[pltpu.VMEM((2, n_slots, d), BF16), sems]),
        out_shape=jax.ShapeDtypeStruct((n, d), F32),
        compiler_params=_params("arbitrary"),
    )(tile_off, n16, ys, comb, x)


def _slab_gains(q_sb, k_sb, q_dil, k_dil):
    ones = jnp.ones((4, HEAD_DIM), F32)
    rep = lambda g: jnp.broadcast_to(g[None, :], (4, HEAD_DIM))
    return jnp.concatenate([rep(q_sb), rep(k_sb), ones, rep(q_dil), rep(k_dil), ones], axis=0)


def kernel(x, positions, norm_mix, w_in, q_norm_sb, k_norm_sb, q_norm_dil, k_norm_dil, out_norm_sb, out_norm_dil, w_out, norm_ffn, w_gate_dense, w_up_dense, w_down_dense, w_router, w_gate_moe, w_up_moe, w_down_moe):
    batch, seq, d = x.shape
    n = batch * seq
    depth = w_in.shape[0]
    xf = x.reshape(n, d)
    pos_b = jnp.broadcast_to(positions.reshape(n, 1), (n, HEAD_DIM))
    cos, sin = rope_tables(pos_b)
    for i in range(depth):
        gains = _slab_gains(q_norm_sb[i], k_norm_sb[i], q_norm_dil[i], k_norm_dil[i])
        qkv = in_projection(xf, norm_mix[i][None, :], layer_weights_bf16(w_in, i), gains, cos, sin,
                            batch)
        o_sb = stick_breaking_attention(qkv)
        o_dil = dilated_attention(qkv)
        j = i // 2
        mix = (o_sb, o_dil, xf, out_norm_sb[i][None, :], out_norm_dil[i][None, :],
               layer_weights_bf16(w_out, i), norm_ffn[i][None, :])
        if i % 2 == 1:
            xf, xn, comb, cnt = out_projection_router(*mix, w_router[j])
            counts = cnt[::V7X_SUBLANES, :N_EXPERTS].astype(jnp.int32)
            xf = moe_ffn(xn, xf, comb, counts, layer_weights_bf16(w_gate_moe, j),
                         layer_weights_bf16(w_up_moe, j), layer_weights_bf16(w_down_moe, j))
        else:
            xf = out_projection_ffn(*mix, layer_weights_bf16(w_gate_dense, j),
                                    layer_weights_bf16(w_up_dense, j),
                                    layer_weights_bf16(w_down_dense, j))
    return xf.reshape(batch, seq, d)
```
